```python
import math
import jax, jax.numpy as jnp
from jax import lax
import numpy as np

D_MODEL = 4096
BATCH = 1
SEQ = 16384
DEPTH = 4
DEC_BATCH = 4
DEC_SEQ = 2048
PAST_LEN = 128

N_MIXERS = 3
HEAD_DIM = 128
ROT_DIM = HEAD_DIM // 4
ROPE_THETA = 500000.0
A_Q_HEADS = D_MODEL // HEAD_DIM
A_KV_HEADS = A_Q_HEADS // 4
A_GROUP = A_Q_HEADS // A_KV_HEADS
A_HALF_WINDOW = 128
A_BLOCK = 128
A_QKV = (A_Q_HEADS + 2 * A_KV_HEADS) * HEAD_DIM
B_HEADS = D_MODEL // (2 * HEAD_DIM)
B_QKV = 3 * B_HEADS * 2 * HEAD_DIM
B_Q_BLOCK = 128
C_HEADS = D_MODEL // HEAD_DIM
C_QKV = 3 * C_HEADS * HEAD_DIM
C_BRANCHES = ((128, 1), (512, 4), (2048, 16))
C_BLOCK = 128
D_FF = ((8 * D_MODEL + 3 * 256 - 1) // (3 * 256)) * 256
DEEPNORM_ALPHA = (2.0 * DEPTH) ** 0.25
DEEPNORM_BETA = (8.0 * DEPTH) ** -0.25
LN_EPS = 1e-5
NEG_INF = -1e30
N_A = (DEPTH + 2) // 3
N_B = (DEPTH + 1) // 3
N_C = DEPTH // 3

kernel_name = "hybrid_bidir_swa_diff_dilated_encoder"


def layer_norm(x, g, b):
    xf = x.astype(jnp.float32)
    mu = jnp.mean(xf, axis=-1, keepdims=True)
    xc = xf - mu
    var = jnp.mean(xc * xc, axis=-1, keepdims=True)
    y = xc * lax.rsqrt(var + LN_EPS) * g.astype(jnp.float32) + b.astype(jnp.float32)
    return y.astype(x.dtype)


def rope_tables(seq):
    pos = jnp.arange(seq, dtype=jnp.float32)
    inv_freq = ROPE_THETA ** (-(jnp.arange(0, ROT_DIM, 2, dtype=jnp.float32) / ROT_DIM))
    ang = pos[:, None] * inv_freq[None, :]
    return jnp.cos(ang), jnp.sin(ang)


def apply_partial_rope(x, cos, sin):
    half = ROT_DIM // 2
    xr = x[..., :ROT_DIM].astype(jnp.float32)
    x1, x2 = xr[..., :half], xr[..., half:]
    c = cos[None, :, None, :]
    s = sin[None, :, None, :]
    rot = jnp.concatenate([x1 * c - x2 * s, x2 * c + x1 * s], axis=-1).astype(x.dtype)
    return jnp.concatenate([rot, x[..., ROT_DIM:]], axis=-1)


def band_attention(q, k, v, half_window, block, sink=None):
    bsz, L, hk, g, dh = q.shape
    nblk = -(-L // block)
    lp = nblk * block
    span = block + 2 * half_window
    q = jnp.pad(q, ((0, 0), (0, lp - L), (0, 0), (0, 0), (0, 0))).reshape(bsz, nblk, block, hk, g, dh)
    kv_pad = ((0, 0), (half_window, lp - L + half_window), (0, 0), (0, 0))
    k = jnp.pad(k, kv_pad)
    v = jnp.pad(v, kv_pad)
    idx = jnp.arange(nblk)[:, None] * block + jnp.arange(span)[None, :]
    kb = k[:, idx]
    vb = v[:, idx]
    qpos = jnp.arange(nblk)[:, None] * block + jnp.arange(block)[None, :]
    kpos = idx - half_window
    valid = ((jnp.abs(qpos[:, :, None] - kpos[:, None, :]) <= half_window)
             & (kpos[:, None, :] >= 0) & (kpos[:, None, :] < L))
    s = jnp.einsum('bnqhgd,bnkhd->bhgnqk', q, kb, preferred_element_type=jnp.float32) * (dh ** -0.5)
    s = jnp.where(valid, s, NEG_INF)
    m = jnp.max(s, axis=-1, keepdims=True)
    if sink is not None:
        sk = sink.astype(jnp.float32).reshape(1, hk, g, 1, 1, 1)
        m_all = jnp.maximum(m, sk)
    else:
        m_all = m
    p = jnp.exp(s - m_all)
    denom_scores = jnp.sum(p, axis=-1, keepdims=True)
    denom = denom_scores + jnp.exp(sk - m_all) if sink is not None else denom_scores
    o = jnp.einsum('bhgnqk,bnkhd->bnqhgd', p, vb, preferred_element_type=jnp.float32)
    o = o / jnp.transpose(denom, (0, 3, 4, 1, 2, 5))
    lse = (m_all + jnp.log(denom_scores))[..., 0]
    lse = jnp.transpose(lse, (0, 3, 4, 1, 2)).reshape(bsz, lp, hk, g)[:, :L]
    o = o.reshape(bsz, lp, hk, g, dh)[:, :L].astype(v.dtype)
    return o, lse


def mixer_a(x, w_qkv, w_o, sink, cos, sin):
    bsz, s_len, _ = x.shape
    qkv = x @ w_qkv
    nq = A_Q_HEADS * HEAD_DIM
    nk = A_KV_HEADS * HEAD_DIM
    q = qkv[..., :nq].reshape(bsz, s_len, A_Q_HEADS, HEAD_DIM)
    k = qkv[..., nq:nq + nk].reshape(bsz, s_len, A_KV_HEADS, HEAD_DIM)
    v = qkv[..., nq + nk:].reshape(bsz, s_len, A_KV_HEADS, HEAD_DIM)
    q = apply_partial_rope(q, cos, sin).reshape(bsz, s_len, A_KV_HEADS, A_GROUP, HEAD_DIM)
    k = apply_partial_rope(k, cos, sin)
    o, _ = band_attention(q, k, v, A_HALF_WINDOW, A_BLOCK, sink.reshape(A_KV_HEADS, A_GROUP))
    return o.reshape(bsz, s_len, nq) @ w_o


def mixer_b(x, w_qkv, w_o, lq1, lk1, lq2, lk2, subln_g, lambda_init, cos, sin):
    bsz, s_len, _ = x.shape
    qkv = x @ w_qkv
    q, k, v = jnp.split(qkv, 3, axis=-1)
    q = apply_partial_rope(q.reshape(bsz, s_len, 2 * B_HEADS, HEAD_DIM), cos, sin)
    k = apply_partial_rope(k.reshape(bsz, s_len, 2 * B_HEADS, HEAD_DIM), cos, sin)
    q = q.reshape(bsz, s_len, B_HEADS, 2, HEAD_DIM)
    k = k.reshape(bsz, s_len, B_HEADS, 2, HEAD_DIM)
    v = v.reshape(bsz, s_len, B_HEADS, 2 * HEAD_DIM)
    f32 = jnp.float32
    lam = (jnp.exp(jnp.sum(lq1.astype(f32) * lk1.astype(f32)))
           - jnp.exp(jnp.sum(lq2.astype(f32) * lk2.astype(f32))) + lambda_init)
    scale = HEAD_DIM ** -0.5
    nblk = s_len // B_Q_BLOCK
    qb = jnp.moveaxis(q.reshape(bsz, nblk, B_Q_BLOCK, B_HEADS, 2, HEAD_DIM), 1, 0)

    def block_fn(qblk):
        sc = jnp.einsum('bqhcd,bkhcd->bhcqk', qblk, k, preferred_element_type=jnp.float32) * scale
        p = jax.nn.softmax(sc, axis=-1)
        a = p[:, :, 0] - lam * p[:, :, 1]
        return jnp.einsum('bhqk,bkhe->bqhe', a, v, preferred_element_type=jnp.float32)

    o = lax.map(block_fn, qb)
    o = jnp.moveaxis(o, 0, 1).reshape(bsz, s_len, B_HEADS, 2 * HEAD_DIM)
    o = o * lax.rsqrt(jnp.mean(o * o, axis=-1, keepdims=True) + LN_EPS)
    o = (o * subln_g.astype(f32) * (1.0 - lambda_init)).astype(x.dtype)
    return o.reshape(bsz, s_len, D_MODEL) @ w_o


def _to_residues(t, dil):
    bsz, s_len = t.shape[:2]
    rest = t.shape[2:]
    t = t.reshape((bsz, s_len // dil, dil) + rest)
    t = jnp.swapaxes(t, 1, 2)
    return t.reshape((bsz * dil, s_len // dil) + rest)


def _from_residues(t, bsz, dil):
    sub = t.shape[1]
    rest = t.shape[2:]
    t = t.reshape((bsz, dil, sub) + rest)
    t = jnp.swapaxes(t, 1, 2)
    return t.reshape((bsz, sub * dil) + rest)


def mixer_c(x, w_qkv, w_o, cos, sin):
    bsz, s_len, _ = x.shape
    qkv = x @ w_qkv
    q, k, v = jnp.split(qkv, 3, axis=-1)
    q = apply_partial_rope(q.reshape(bsz, s_len, C_HEADS, HEAD_DIM), cos, sin)
    k = apply_partial_rope(k.reshape(bsz, s_len, C_HEADS, HEAD_DIM), cos, sin)
    v = v.reshape(bsz, s_len, C_HEADS, HEAD_DIM)
    outs, lses = [], []
    for window, dil in C_BRANCHES:
        half = window // (2 * dil)
        qd = _to_residues(q, dil)[:, :, :, None, :]
        o, lse = band_attention(qd, _to_residues(k, dil), _to_residues(v, dil), half, C_BLOCK)
        outs.append(_from_residues(o[:, :, :, 0], bsz, dil))
        lses.append(_from_residues(lse[:, :, :, 0], bsz, dil))
    wts = jax.nn.softmax(jnp.stack(lses, axis=0), axis=0)
    o = jnp.sum(wts[..., None] * jnp.stack(outs, axis=0).astype(jnp.float32), axis=0).astype(x.dtype)
    return o.reshape(bsz, s_len, D_MODEL) @ w_o


def swiglu(x, w_gate, w_up, w_down):
    return (jax.nn.silu(x @ w_gate) * (x @ w_up)) @ w_down


def trunk(x, a_w_qkv, a_w_o, a_sink, b_w_qkv, b_w_o, b_lambda_q1, b_lambda_k1, b_lambda_q2,
          b_lambda_k2, b_subln_g, c_w_qkv, c_w_o, ln1_g, ln1_b, ln2_g, ln2_b, w_gate, w_up, w_down):
    cos, sin = rope_tables(x.shape[1])
    for i in range(DEPTH):
        kind, j = i % N_MIXERS, i // N_MIXERS
        if kind == 0:
            h = mixer_a(x, a_w_qkv[j], a_w_o[j], a_sink[j], cos, sin)
        elif kind == 1:
            lambda_init = 0.8 - 0.6 * math.exp(-0.3 * i)
            h = mixer_b(x, b_w_qkv[j], b_w_o[j], b_lambda_q1[j], b_lambda_k1[j], b_lambda_q2[j],
                        b_lambda_k2[j], b_subln_g[j], lambda_init, cos, sin)
        else:
            h = mixer_c(x, c_w_qkv[j], c_w_o[j], cos, sin)
        x = layer_norm(DEEPNORM_ALPHA * x + h, ln1_g[i], ln1_b[i])
        x = layer_norm(DEEPNORM_ALPHA * x + swiglu(x, w_gate[i], w_up[i], w_down[i]), ln2_g[i], ln2_b[i])
    return x


def setup_inputs(seed: int = 0) -> dict:
    key = jax.random.key(seed)
    ks = jax.random.split(key, 24)
    f32 = jnp.float32
    beta = DEEPNORM_BETA
    d_in = D_MODEL ** -0.5

    def nrm(k, shape, scale):
        return jax.random.normal(k, shape, f32) * scale

    def col_scale(n_qk, n_v):
        return jnp.concatenate([jnp.ones((n_qk,), f32), jnp.full((n_v,), beta, f32)])

    a_v = A_KV_HEADS * HEAD_DIM
    a_w_qkv = nrm(ks[2], (N_A, D_MODEL, A_QKV), d_in) * col_scale(A_QKV - a_v, a_v)
    a_w_o = nrm(ks[3], (N_A, A_Q_HEADS * HEAD_DIM, D_MODEL), (A_Q_HEADS * HEAD_DIM) ** -0.5 * beta)
    a_sink = nrm(ks[4], (N_A, A_Q_HEADS), 0.5)
    b_v = B_QKV // 3
    b_w_qkv = nrm(ks[5], (N_B, D_MODEL, B_QKV), d_in) * col_scale(B_QKV - b_v, b_v)
    b_w_o = nrm(ks[6], (N_B, D_MODEL, D_MODEL), d_in * beta)
    b_lambda_q1 = nrm(ks[7], (N_B, HEAD_DIM), 0.1)
    b_lambda_k1 = nrm(ks[8], (N_B, HEAD_DIM), 0.1)
    b_lambda_q2 = nrm(ks[9], (N_B, HEAD_DIM), 0.1)
    b_lambda_k2 = nrm(ks[10], (N_B, HEAD_DIM), 0.1)
    b_subln_g = 1.0 + nrm(ks[11], (N_B, 2 * HEAD_DIM), 0.02)
    c_v = C_QKV // 3
    c_w_qkv = nrm(ks[12], (N_C, D_MODEL, C_QKV), d_in) * col_scale(C_QKV - c_v, c_v)
    c_w_o = nrm(ks[13], (N_C, D_MODEL, D_MODEL), d_in * beta)
    ln1_g = 1.0 + nrm(ks[14], (DEPTH, D_MODEL), 0.02)
    ln1_b = nrm(ks[15], (DEPTH, D_MODEL), 0.02)
    ln2_g = 1.0 + nrm(ks[16], (DEPTH, D_MODEL), 0.02)
    ln2_b = nrm(ks[17], (DEPTH, D_MODEL), 0.02)
    w_gate = nrm(ks[18], (DEPTH, D_MODEL, D_FF), d_in)
    w_up = nrm(ks[19], (DEPTH, D_MODEL, D_FF), d_in * beta)
    w_down = nrm(ks[20], (DEPTH, D_FF, D_MODEL), D_FF ** -0.5 * beta)
    x_prompt = jax.random.normal(ks[0], (BATCH, SEQ, D_MODEL), f32)
    x_sample = jax.random.normal(ks[1], (DEC_BATCH, DEC_SEQ, D_MODEL), f32)
    return {"x_prompt": x_prompt, "x_sample": x_sample,
            "a_w_qkv": a_w_qkv, "a_w_o": a_w_o, "a_sink": a_sink,
            "b_w_qkv": b_w_qkv, "b_w_o": b_w_o, "b_lambda_q1": b_lambda_q1,
            "b_lambda_k1": b_lambda_k1, "b_lambda_q2": b_lambda_q2, "b_lambda_k2": b_lambda_k2,
            "b_subln_g": b_subln_g, "c_w_qkv": c_w_qkv, "c_w_o": c_w_o,
            "ln1_g": ln1_g, "ln1_b": ln1_b, "ln2_g": ln2_g, "ln2_b": ln2_b,
            "w_gate": w_gate, "w_up": w_up, "w_down": w_down}


def reference(x_prompt, x_sample, a_w_qkv, a_w_o, a_sink, b_w_qkv, b_w_o, b_lambda_q1, b_lambda_k1,
              b_lambda_q2, b_lambda_k2, b_subln_g, c_w_qkv, c_w_o, ln1_g, ln1_b, ln2_g, ln2_b,
              w_gate, w_up, w_down):
    y_prompt = trunk(x_prompt, a_w_qkv, a_w_o, a_sink, b_w_qkv, b_w_o, b_lambda_q1, b_lambda_k1,
                     b_lambda_q2, b_lambda_k2, b_subln_g, c_w_qkv, c_w_o, ln1_g, ln1_b, ln2_g, ln2_b,
                     w_gate, w_up, w_down)
    y_sample = trunk(x_sample, a_w_qkv, a_w_o, a_sink, b_w_qkv, b_w_o, b_lambda_q1, b_lambda_k1,
                     b_lambda_q2, b_lambda_k2, b_subln_g, c_w_qkv, c_w_o, ln1_g, ln1_b, ln2_g, ln2_b,
                     w_gate, w_up, w_down)
    return (y_prompt, y_sample)
```

```python
import functools
import math

import numpy as np
import jax
import jax.numpy as jnp
from jax import lax
from jax.experimental import pallas as pl
from jax.experimental.pallas import tpu as pltpu

F32 = jnp.float32
BF16 = jnp.bfloat16

HEAD_DIM = 128
ROT_DIM = HEAD_DIM // 4
ROPE_THETA = 500000.0
LN_EPS = 1e-5
NEG_INF = -1e30
N_MIXERS = 3
A_GROUP = 4
A_HALF_WINDOW = 128
C_BRANCHES = ((128, 1), (512, 4), (2048, 16))
VMEM_LIMIT_BYTES = 56 * 1024 * 1024

_NT = (((1,), (1,)), ((), ()))


def _params(*semantics):
    return pltpu.CompilerParams(dimension_semantics=semantics, vmem_limit_bytes=VMEM_LIMIT_BYTES)


def _tile(n, prefs):
    for t in prefs:
        if n % t == 0:
            return t
    raise ValueError(f"no tile in {prefs} divides {n}")


def _proj_rope_kernel(x_ref, w_ref, t_ref, o_ref):
    acc = jnp.dot(x_ref[...], w_ref[...], preferred_element_type=F32)
    cos, s1, s2 = t_ref[0, 0], t_ref[0, 1], t_ref[0, 2]
    half = ROT_DIM // 2
    for c in range(o_ref.shape[0]):
        blk = acc[:, c * HEAD_DIM:(c + 1) * HEAD_DIM]
        blk = (blk * cos + pltpu.roll(blk, HEAD_DIM - half, 1) * s1 + pltpu.roll(blk, half, 1) * s2)
        o_ref[c] = blk.astype(BF16)


def _proj_plain_kernel(x_ref, w_ref, o_ref):
    acc = jnp.dot(x_ref[...], w_ref[...], preferred_element_type=F32)
    chunk = o_ref.shape[2]
    for c in range(o_ref.shape[0]):
        o_ref[c] = acc[:, c * chunk:(c + 1) * chunk].astype(BF16)


def _proj_heads(xb, w, tab=None, n_q_cols=0, chunk=HEAD_DIM):
    m, k = xb.shape
    n = w.shape[1]
    tm = _tile(m, (1024, 512, 256))
    tn = _tile(math.gcd(n, n_q_cols) if n_q_cols else n, (1024, 512, 256, 128))
    grid = (m // tm, n // tn)
    in_specs = [pl.BlockSpec((tm, k), lambda i, j: (i, 0)),
                pl.BlockSpec((k, tn), lambda i, j: (0, j))]
    args = [xb, w]
    if tab is not None:
        assert chunk == HEAD_DIM
        n_q_tiles = n_q_cols // tn
        in_specs.append(pl.BlockSpec((1, 3, tm, HEAD_DIM),
                                     lambda i, j: (jnp.where(j < n_q_tiles, 0, 1), 0, i, 0)))
        args.append(tab)
        body = _proj_rope_kernel
    else:
        body = _proj_plain_kernel
    return pl.pallas_call(
        body,
        grid=grid,
        in_specs=in_specs,
        out_specs=pl.BlockSpec((tn // chunk, tm, chunk), lambda i, j: (j, i, 0)),
        out_shape=jax.ShapeDtypeStruct((n // chunk, m, chunk), BF16),
        compiler_params=_params("parallel", "parallel"),
    )(*args)


def _mm_kernel(a_ref, w_ref, o_ref):
    o_ref[...] = jnp.dot(a_ref[...], w_ref[...], preferred_element_type=F32).astype(o_ref.dtype)


def _matmul(a, w, out_dtype=F32):
    m, k = a.shape
    n = w.shape[1]
    tm = _tile(m, (1024, 512, 256))
    tn = _tile(n, (1024, 512, 256, 128))
    return pl.pallas_call(
        _mm_kernel,
        grid=(m // tm, n // tn),
        in_specs=[pl.BlockSpec((tm, k), lambda i, j: (i, 0)),
                  pl.BlockSpec((k, tn), lambda i, j: (0, j))],
        out_specs=pl.BlockSpec((tm, tn), lambda i, j: (i, j)),
        out_shape=jax.ShapeDtypeStruct((m, n), out_dtype),
        compiler_params=_params("parallel", "parallel"),
    )(a, w)


def _gateup_kernel(x_ref, wg_ref, wu_ref, o_ref):
    x = x_ref[...]
    g = jnp.dot(x, wg_ref[...], preferred_element_type=F32)
    u = jnp.dot(x, wu_ref[...], preferred_element_type=F32)
    o_ref[...] = (g * jax.nn.sigmoid(g) * u).astype(BF16)


def _gateup(xb, wg, wu):
    m, k = xb.shape
    f = wg.shape[1]
    tm = _tile(m, (1024, 512, 256))
    tn = _tile(f, (512, 256, 128))
    return pl.pallas_call(
        _gateup_kernel,
        grid=(m // tm, f // tn),
        in_specs=[pl.BlockSpec((tm, k), lambda i, j: (i, 0)),
                  pl.BlockSpec((k, tn), lambda i, j: (0, j)),
                  pl.BlockSpec((k, tn), lambda i, j: (0, j))],
        out_specs=pl.BlockSpec((tm, tn), lambda i, j: (i, j)),
        out_shape=jax.ShapeDtypeStruct((m, f), BF16),
        compiler_params=_params("parallel", "parallel"),
    )(xb, wg, wu)


def _down_kernel(h_ref, w_ref, o_ref, *, n_split):
    k = pl.program_id(1)
    h = h_ref[...]
    tn = o_ref.shape[1] // n_split
    for c in range(n_split):
        cols = slice(c * tn, (c + 1) * tn)
        part = jnp.dot(h, w_ref[:, cols], preferred_element_type=F32)

        @pl.when(k == 0)
        def _():
            o_ref[:, cols] = part

        @pl.when(k > 0)
        def _():
            o_ref[:, cols] += part


def _down(h, w):
    m, f = h.shape
    n = w.shape[1]
    tm = _tile(m, (1024, 512, 256))
    tk = _tile(f, (512, 256, 128))
    n_split = max(1, n // 1024)
    return pl.pallas_call(
        functools.partial(_down_kernel, n_split=n_split),
        grid=(m // tm, f // tk),
        in_specs=[pl.BlockSpec((tm, tk), lambda i, k: (i, k)),
                  pl.BlockSpec((tk, n), lambda i, k: (k, 0))],
        out_specs=pl.BlockSpec((tm, n), lambda i, k: (i, 0)),
        out_shape=jax.ShapeDtypeStruct((m, n), F32),
        compiler_params=_params("parallel", "arbitrary"),
    )(h, w)


def _ln_kernel(x_ref, h_ref, g_ref, b_ref, xo_ref, xb_ref, *, alpha):
    y = alpha * x_ref[...] + h_ref[...]
    mu = jnp.mean(y, axis=-1, keepdims=True)
    yc = y - mu
    var = jnp.mean(yc * yc, axis=-1, keepdims=True)
    out = yc * lax.rsqrt(var + LN_EPS) * g_ref[...] + b_ref[...]
    xo_ref[...] = out
    xb_ref[...] = out.astype(BF16)


def _deepnorm_ln(x, h, g, b, alpha):
    m, d = x.shape
    tm = _tile(m, (256, 128))
    row = pl.BlockSpec((tm, d), lambda i: (i, 0))
    vec = pl.BlockSpec((1, d), lambda i: (0, 0))
    return pl.pallas_call(
        functools.partial(_ln_kernel, alpha=alpha),
        grid=(m // tm,),
        in_specs=[row, row, vec, vec],
        out_specs=[row, row],
        out_shape=[jax.ShapeDtypeStruct((m, d), F32), jax.ShapeDtypeStruct((m, d), BF16)],
        compiler_params=_params("parallel"),
    )(x, h, g.reshape(1, d), b.reshape(1, d))


def _window_attn_kernel(sink_ref, q_ref, k_ref, v_ref, bias_ref, o_ref, s_scr, *,
                        seq_len, tq, chunks, has_sink):
    head = pl.program_id(1)

    def body(qi, carry):
        q_start = pl.multiple_of(qi * tq, tq)
        q = q_ref[0, pl.ds(q_start, tq), :]
        starts = []
        col = 0
        for rel, size in chunks:
            start = q_start + rel
            inside = jnp.logical_and(start >= 0, start + size <= seq_len)
            cstart = pl.multiple_of(jnp.clip(start, 0, seq_len - size), HEAD_DIM)
            starts.append(cstart)
            kj = k_ref[0, pl.ds(cstart, size), :]
            sj = lax.dot_general(q, kj, _NT, preferred_element_type=F32)
            pen = jnp.where(inside, 0.0, NEG_INF).astype(F32)
            s_scr[:, col:col + size] = sj + bias_ref[:, col:col + size] + pen
            col += size
        s = s_scr[...]
        m = jnp.max(s, axis=-1, keepdims=True)
        if has_sink:
            sk = sink_ref[head]
            m = jnp.maximum(m, sk)
        p = jnp.exp(s - m)
        denom = jnp.sum(p, axis=-1, keepdims=True)
        if has_sink:
            denom = denom + jnp.exp(sk - m)
        pb = p.astype(BF16)
        acc = jnp.zeros((tq, HEAD_DIM), F32)
        col = 0
        for (rel, size), cstart in zip(chunks, starts):
            vj = v_ref[0, pl.ds(cstart, size), :]
            acc = acc + jnp.dot(pb[:, col:col + size], vj, preferred_element_type=F32)
            col += size
        o_ref[pl.ds(q_start, tq), :] = (acc / denom).astype(BF16)
        return carry

    lax.fori_loop(0, seq_len // tq, body, 0)


def _window_attn(qk, v, sink, bias, *, n_q_heads, group, chunks, tq, seq_len, n_seq, row_block0):
    has_sink = sink is not None
    if not has_sink:
        sink = jnp.zeros((n_q_heads,), F32)
    span = bias.shape[1]
    seq_spec = lambda head_of: pl.BlockSpec((1, seq_len, HEAD_DIM),
                                            lambda b, h: (head_of(h), row_block0 + b, 0))
    return pl.pallas_call(
        functools.partial(_window_attn_kernel, seq_len=seq_len, tq=tq, chunks=chunks, has_sink=has_sink),
        grid=(n_seq, n_q_heads),
        in_specs=[pl.BlockSpec(memory_space=pltpu.SMEM),
                  seq_spec(lambda h: h),
                  seq_spec(lambda h: n_q_heads + h // group),
                  seq_spec(lambda h: h // group),
                  pl.BlockSpec((tq, span), lambda b, h: (0, 0))],
        out_specs=pl.BlockSpec((seq_len, HEAD_DIM), lambda b, h: (b, h)),
        out_shape=jax.ShapeDtypeStruct((n_seq * seq_len, n_q_heads * HEAD_DIM), BF16),
        scratch_shapes=[pltpu.VMEM((tq, span), F32)],
        compiler_params=_params("parallel", "parallel"),
    )(sink, qk, qk, v, bias)


def _band_bias(tq, chunks, mult_of_offset):
    rows = np.arange(tq)[:, None]
    cols = np.concatenate([rel + np.arange(size) for rel, size in chunks])[None, :]
    mult = mult_of_offset(rows - cols)
    return jnp.asarray(np.where(mult > 0, np.log(np.maximum(mult, 1).astype(np.float64)), NEG_INF), F32)


def _mult_a(d):
    return (np.abs(d) <= A_HALF_WINDOW).astype(np.int64)


def _mult_c(d):
    mult = np.zeros_like(d)
    for window, dil in C_BRANCHES:
        reach = (window // (2 * dil)) * dil
        mult = mult + ((np.abs(d) <= reach) & (d % dil == 0))
    return mult


def _diff_attn_kernel(q_ref, k_ref, v_ref, lq1_ref, lk1_ref, lq2_ref, lk2_ref, g_ref, o_ref,
                      acc_scr, m_scr, l_scr, *, seq_len, tk, lambda_init):
    m_scr[...] = jnp.full(m_scr.shape, NEG_INF, F32)
    l_scr[...] = jnp.zeros(l_scr.shape, F32)
    acc_scr[...] = jnp.zeros(acc_scr.shape, F32)

    def body(kj, carry):
        k_start = pl.multiple_of(kj * tk, tk)
        v = v_ref[0, pl.ds(k_start, tk), :]
        for c in range(2):
            s = lax.dot_general(q_ref[c], k_ref[c, pl.ds(k_start, tk), :], _NT,
                                preferred_element_type=F32)
            m_old = m_scr[c]
            m_new = jnp.maximum(m_old, jnp.max(s, axis=-1, keepdims=True))
            rescale = jnp.exp(m_old - m_new)
            p = jnp.exp(s - m_new)
            l_scr[c] = rescale * l_scr[c] + jnp.sum(p, axis=-1, keepdims=True)
            acc_scr[c] = rescale * acc_scr[c] + jnp.dot(p.astype(BF16), v, preferred_element_type=F32)
            m_scr[c] = m_new
        return carry

    lax.fori_loop(0, seq_len // tk, body, 0)

    lam = (jnp.exp(jnp.sum(lq1_ref[...] * lk1_ref[...], keepdims=True))
           - jnp.exp(jnp.sum(lq2_ref[...] * lk2_ref[...], keepdims=True)) + lambda_init)
    o = acc_scr[0] / l_scr[0] - lam * (acc_scr[1] / l_scr[1])
    o = o * lax.rsqrt(jnp.mean(o * o, axis=-1, keepdims=True) + LN_EPS)
    o_ref[...] = (o * g_ref[...] * (1.0 - lambda_init)).astype(BF16)


def _diff_attn(qk, v, lq1, lk1, lq2, lk2, subln_g, lambda_init, *, n_heads, seq_len, n_seq, row_block0):
    tq = _tile(seq_len, (512, 256, 128))
    tk = _tile(seq_len, (1024, 512, 256, 128))
    q_blocks = seq_len // tq
    vd = 2 * HEAD_DIM
    vec = lambda width: pl.BlockSpec((1, width), lambda b, h, i: (0, 0))
    return pl.pallas_call(
        functools.partial(_diff_attn_kernel, seq_len=seq_len, tk=tk, lambda_init=lambda_init),
        grid=(n_seq, n_heads, q_blocks),
        in_specs=[pl.BlockSpec((2, tq, HEAD_DIM), lambda b, h, i: (h, (row_block0 + b) * q_blocks + i, 0)),
                  pl.BlockSpec((2, seq_len, HEAD_DIM), lambda b, h, i: (n_heads + h, row_block0 + b, 0)),
                  pl.BlockSpec((1, seq_len, vd), lambda b, h, i: (h, row_block0 + b, 0)),
                  vec(HEAD_DIM), vec(HEAD_DIM), vec(HEAD_DIM), vec(HEAD_DIM), vec(vd)],
        out_specs=pl.BlockSpec((tq, vd), lambda b, h, i: (b * q_blocks + i, h)),
        out_shape=jax.ShapeDtypeStruct((n_seq * seq_len, n_heads * vd), BF16),
        scratch_shapes=[pltpu.VMEM((2, tq, vd), F32), pltpu.VMEM((2, tq, 1), F32), pltpu.VMEM((2, tq, 1), F32)],
        compiler_params=_params("parallel", "parallel", "arbitrary"),
    )(qk, qk, v, lq1.reshape(1, -1), lk1.reshape(1, -1), lq2.reshape(1, -1), lk2.reshape(1, -1),
      subln_g.reshape(1, -1))


def _rope_tables(positions, q_scale):
    half = ROT_DIM // 2
    inv_freq = ROPE_THETA ** (-(jnp.arange(0, ROT_DIM, 2, dtype=F32) / ROT_DIM))
    ang = positions[:, None] * inv_freq[None, :]
    cos, sin = jnp.cos(ang), jnp.sin(ang)
    m = positions.shape[0]
    zero_half = jnp.zeros((m, half), F32)
    rest = HEAD_DIM - ROT_DIM
    c = jnp.concatenate([cos, cos, jnp.ones((m, rest), F32)], axis=1)
    s1 = jnp.concatenate([-sin, zero_half, jnp.zeros((m, rest), F32)], axis=1)
    s2 = jnp.concatenate([zero_half, sin, jnp.zeros((m, rest), F32)], axis=1)
    tab = jnp.stack([c, s1, s2])
    return jnp.stack([tab * q_scale, tab])


def _seq_groups(seq_lens_and_counts):
    groups, row = [], 0
    for seq_len, n_seq in seq_lens_and_counts:
        assert row % seq_len == 0
        groups.append((seq_len, n_seq, row // seq_len))
        row += seq_len * n_seq
    return groups


def kernel(x_prompt, x_sample, a_w_qkv, a_w_o, a_sink, b_w_qkv, b_w_o, b_lambda_q1, b_lambda_k1,
           b_lambda_q2, b_lambda_k2, b_subln_g, c_w_qkv, c_w_o, ln1_g, ln1_b, ln2_g, ln2_b,
           w_gate, w_up, w_down):
    d = x_prompt.shape[-1]
    depth = ln1_g.shape[0]
    n_heads = d // HEAD_DIM
    n_kv_a = n_heads // A_GROUP
    b_heads = n_heads // 2
    d_ff = w_gate.shape[-1]
    alpha = (2.0 * depth) ** 0.25
    q_scale = HEAD_DIM ** -0.5

    groups = _seq_groups([(x_prompt.shape[1], x_prompt.shape[0]), (x_sample.shape[1], x_sample.shape[0])])
    x = jnp.concatenate([x_prompt.reshape(-1, d), x_sample.reshape(-1, d)], axis=0)
    xb = x.astype(BF16)
    positions = jnp.concatenate([jnp.tile(jnp.arange(seq_len, dtype=F32), n_seq)
                                 for seq_len, n_seq, _ in groups])
    tab = _rope_tables(positions, q_scale)

    tq_a, tq_c = 256, 256
    chunks_a = ((-A_HALF_WINDOW, A_HALF_WINDOW), (0, tq_a), (tq_a, A_HALF_WINDOW))
    reach_c = max((w // (2 * dil)) * dil for w, dil in C_BRANCHES)
    chunks_c = tuple((rel, tq_c) for rel in range(-reach_c, tq_c + reach_c, tq_c))
    bias_a = _band_bias(tq_a, chunks_a, _mult_a)
    bias_c = _band_bias(tq_c, chunks_c, _mult_c)

    f_pad = -(-d_ff // 512) * 512

    for i in range(depth):
        kind, j = i % N_MIXERS, i // N_MIXERS
        if kind == 0:
            nq = n_heads * HEAD_DIM
            nk = n_kv_a * HEAD_DIM
            w_qkv = a_w_qkv[j]
            qk = _proj_heads(xb, w_qkv[:, :nq + nk].astype(BF16), tab, n_q_cols=nq)
            v = _proj_heads(xb, w_qkv[:, nq + nk:].astype(BF16))
            attn = [_window_attn(qk, v, a_sink[j], bias_a, n_q_heads=n_heads, group=A_GROUP,
                                 chunks=chunks_a, tq=tq_a, seq_len=sl, n_seq=ns, row_block0=rb)
                    for sl, ns, rb in groups]
            w_o = a_w_o[j]
        elif kind == 1:
            lambda_init = 0.8 - 0.6 * math.exp(-0.3 * i)
            w_qkv = b_w_qkv[j]
            qk = _proj_heads(xb, w_qkv[:, :2 * d].astype(BF16), tab, n_q_cols=d)
            v = _proj_heads(xb, w_qkv[:, 2 * d:].astype(BF16), chunk=2 * HEAD_DIM)
            attn = [_diff_attn(qk, v, b_lambda_q1[j], b_lambda_k1[j], b_lambda_q2[j], b_lambda_k2[j],
                               b_subln_g[j], lambda_init, n_heads=b_heads, seq_len=sl, n_seq=ns,
                               row_block0=rb)
                    for sl, ns, rb in groups]
            w_o = b_w_o[j]
        else:
            w_qkv = c_w_qkv[j]
            qk = _proj_heads(xb, w_qkv[:, :2 * d].astype(BF16), tab, n_q_cols=d)
            v = _proj_heads(xb, w_qkv[:, 2 * d:].astype(BF16))
            attn = [_window_attn(qk, v, None, bias_c, n_q_heads=n_heads, group=1,
                                 chunks=chunks_c, tq=tq_c, seq_len=sl, n_seq=ns, row_block0=rb)
                    for sl, ns, rb in groups]
            w_o = c_w_o[j]
        h = _matmul(jnp.concatenate(attn, axis=0), w_o.astype(BF16))
        x, xb = _deepnorm_ln(x, h, ln1_g[i], ln1_b[i], alpha)

        pad = ((0, 0), (0, f_pad - d_ff))
        hidden = _gateup(xb, jnp.pad(w_gate[i].astype(BF16), pad), jnp.pad(w_up[i].astype(BF16), pad))
        h = _down(hidden, jnp.pad(w_down[i].astype(BF16), (pad[1], pad[0])))
        x, xb = _deepnorm_ln(x, h, ln2_g[i], ln2_b[i], alpha)

    n_prompt = x_prompt.shape[0] * x_prompt.shape[1]
    return x[:n_prompt].reshape(x_prompt.shape), x[n_prompt:].reshape(x_sample.shape)
```

```python
import functools
import math

import numpy as np
import jax
import jax.numpy as jnp
from jax import lax
from jax.experimental import pallas as pl
from jax.experimental.pallas import tpu as pltpu

F32 = jnp.float32
BF16 = jnp.bfloat16

LANE = 128
HEAD_DIM = 128
LOG2_E = math.log2(math.e)
ROT_DIM = HEAD_DIM // 4
ROPE_THETA = 500000.0
LN_EPS = 1e-5
NEG_INF = -1e30
N_MIXERS = 3
A_GROUP = 4
A_HALF_WINDOW = 128
C_BRANCHES = ((128, 1), (512, 4), (2048, 16))
VMEM_LIMIT_BYTES = 56 * 1024 * 1024

ROWS_PER_PASS = 256
_NT = (((1,), (1,)), ((), ()))


def _params(*semantics):
    return pltpu.CompilerParams(dimension_semantics=semantics, vmem_limit_bytes=VMEM_LIMIT_BYTES)


def _tile(n, prefs):
    for t in prefs:
        if n % t == 0:
            return t
    raise ValueError(f"no tile in {prefs} divides {n}")


def _proj_rope_kernel(x_ref, w_ref, t_ref, o_ref):
    acc = jnp.dot(x_ref[...], w_ref[...], preferred_element_type=F32)
    cos, s1, s2 = t_ref[0, 0], t_ref[0, 1], t_ref[0, 2]
    half = ROT_DIM // 2
    for c in range(o_ref.shape[0]):
        blk = acc[:, c * HEAD_DIM:(c + 1) * HEAD_DIM]
        blk = (blk * cos + pltpu.roll(blk, HEAD_DIM - half, 1) * s1 + pltpu.roll(blk, half, 1) * s2)
        o_ref[c] = blk.astype(BF16)


def _proj_plain_kernel(x_ref, w_ref, o_ref):
    acc = jnp.dot(x_ref[...], w_ref[...], preferred_element_type=F32)
    chunk = o_ref.shape[2]
    for c in range(o_ref.shape[0]):
        o_ref[c] = acc[:, c * chunk:(c + 1) * chunk].astype(BF16)


def _proj_heads(xb, w, tab=None, n_q_cols=0, chunk=HEAD_DIM):
    m, k = xb.shape
    n = w.shape[1]
    tm = _tile(m, (1024, 512, 256))
    tn = _tile(math.gcd(n, n_q_cols) if n_q_cols else n, (1024, 512, 256, 128))
    grid = (m // tm, n // tn)
    in_specs = [pl.BlockSpec((tm, k), lambda i, j: (i, 0)),
                pl.BlockSpec((k, tn), lambda i, j: (0, j))]
    args = [xb, w]
    if tab is not None:
        assert chunk == HEAD_DIM
        n_q_tiles = n_q_cols // tn
        in_specs.append(pl.BlockSpec((1, 3, tm, HEAD_DIM),
                                     lambda i, j: (jnp.where(j < n_q_tiles, 0, 1), 0, i, 0)))
        args.append(tab)
        body = _proj_rope_kernel
    else:
        body = _proj_plain_kernel
    return pl.pallas_call(
        body,
        grid=grid,
        in_specs=in_specs,
        out_specs=pl.BlockSpec((tn // chunk, tm, chunk), lambda i, j: (j, i, 0)),
        out_shape=jax.ShapeDtypeStruct((n // chunk, m, chunk), BF16),
        compiler_params=_params("parallel", "parallel"),
        name="proj_heads_rope" if tab is not None else "proj_heads",
    )(*args)


def _mm_kernel(a_ref, w_ref, o_ref):
    o_ref[...] = jnp.dot(a_ref[...], w_ref[...], preferred_element_type=F32).astype(o_ref.dtype)


def _matmul(a, w, out_dtype=F32):
    m, k = a.shape
    n = w.shape[1]
    max_tile = 1024 if 8 * 1024 * k <= VMEM_LIMIT_BYTES * 3 // 4 else 512
    tm = _tile(m, tuple(t for t in (1024, 512, 256) if t <= max_tile))
    tn = _tile(n, tuple(t for t in (1024, 512, 256, 128) if t <= max_tile))
    return pl.pallas_call(
        _mm_kernel,
        grid=(m // tm, n // tn),
        in_specs=[pl.BlockSpec((tm, k), lambda i, j: (i, 0)),
                  pl.BlockSpec((k, tn), lambda i, j: (0, j))],
        out_specs=pl.BlockSpec((tm, tn), lambda i, j: (i, j)),
        out_shape=jax.ShapeDtypeStruct((m, n), out_dtype),
        compiler_params=_params("parallel", "parallel"),
        name="matmul",
    )(a, w)


def _gateup_kernel(x_ref, wg_ref, wu_ref, o_ref):
    x = x_ref[...]
    g = jnp.dot(x, wg_ref[...], preferred_element_type=F32)
    u = jnp.dot(x, wu_ref[...], preferred_element_type=F32)
    o_ref[...] = (g * jax.nn.sigmoid(g) * u).astype(BF16)


def _gateup(xb, wg, wu):
    m, k = xb.shape
    f = wg.shape[1]
    tn = _tile(f, (512, 256, 128))
    tm = _tile(m, (2048, 1024, 512, 256) if tn <= 256 else (1024, 512, 256))
    return pl.pallas_call(
        _gateup_kernel,
        grid=(m // tm, f // tn),
        in_specs=[pl.BlockSpec((tm, k), lambda i, j: (i, 0)),
                  pl.BlockSpec((k, tn), lambda i, j: (0, j)),
                  pl.BlockSpec((k, tn), lambda i, j: (0, j))],
        out_specs=pl.BlockSpec((tm, tn), lambda i, j: (i, j)),
        out_shape=jax.ShapeDtypeStruct((m, f), BF16),
        compiler_params=_params("parallel", "parallel"),
        name="ffn_gate_up",
    )(xb, wg, wu)


def _ln_kernel(x_ref, h_ref, g_ref, b_ref, xo_ref, xb_ref, *, alpha):
    y = alpha * x_ref[...] + h_ref[...]
    mu = jnp.mean(y, axis=-1, keepdims=True)
    yc = y - mu
    var = jnp.mean(yc * yc, axis=-1, keepdims=True)
    out = yc * lax.rsqrt(var + LN_EPS) * g_ref[...] + b_ref[...]
    xo_ref[...] = out
    xb_ref[...] = out.astype(BF16)


def _deepnorm_ln(x, h, g, b, alpha):
    m, d = x.shape
    tm = _tile(m, (256, 128))
    row = pl.BlockSpec((tm, d), lambda i: (i, 0))
    vec = pl.BlockSpec((1, d), lambda i: (0, 0))
    return pl.pallas_call(
        functools.partial(_ln_kernel, alpha=alpha),
        grid=(m // tm,),
        in_specs=[row, row, vec, vec],
        out_specs=[row, row],
        out_shape=[jax.ShapeDtypeStruct((m, d), F32), jax.ShapeDtypeStruct((m, d), BF16)],
        compiler_params=_params("parallel"),
        name="deepnorm_ln",
    )(x, h, g.reshape(1, d), b.reshape(1, d))


def _window_attn_kernel(sink_ref, q_ref, k_ref, v_ref, bias_ref, o_ref, s_scr, p_scr, *,
                        seq_len, tq, chunks, has_sink):
    head = pl.program_id(1)
    n_sub = s_scr.shape[0]
    n_lane_blocks = s_scr.shape[2] // LANE

    def scores(sub, q_start):
        q = q_ref[0, pl.ds(q_start, tq), :]
        starts = []
        col = 0
        for rel, size in chunks:
            start = q_start + rel
            inside = jnp.logical_and(start >= 0, start + size <= seq_len)
            cstart = pl.multiple_of(jnp.clip(start, 0, seq_len - size), HEAD_DIM)
            starts.append(cstart)
            kj = k_ref[0, pl.ds(cstart, size), :]
            sj = lax.dot_general(q, kj, _NT, preferred_element_type=F32)
            pen = jnp.where(inside, 0.0, NEG_INF).astype(F32)
            s_scr[sub, :, col:col + size] = sj + bias_ref[:, col:col + size] + pen
            col += size
        return starts

    def softmax(sub):
        denoms = []
        for r0 in range(0, tq, ROWS_PER_PASS):
            rows = slice(r0, r0 + ROWS_PER_PASS)
            m = s_scr[sub, rows, 0:LANE]
            for cb in range(1, n_lane_blocks):
                m = jnp.maximum(m, s_scr[sub, rows, cb * LANE:(cb + 1) * LANE])
            m = jnp.broadcast_to(jnp.max(m, axis=-1, keepdims=True), (ROWS_PER_PASS, LANE))
            if has_sink:
                sk = sink_ref[head] * LOG2_E
                m = jnp.maximum(m, sk)
            lsum = jnp.zeros((ROWS_PER_PASS, LANE), F32)
            for cb in range(n_lane_blocks):
                cols = slice(cb * LANE, (cb + 1) * LANE)
                p = jnp.exp2(s_scr[sub, rows, cols] - m)
                lsum = lsum + p
                p_scr[sub, rows, cols] = p.astype(BF16)
            denom = jnp.sum(lsum, axis=-1, keepdims=True)
            if has_sink:
                denom = denom + jnp.exp2(sk - m[:, 0:1])
            denoms.append(denom)
        return jnp.concatenate(denoms, axis=0)

    def weighted_values(sub, q_start, starts, denom):
        acc = jnp.zeros((tq, HEAD_DIM), F32)
        col = 0
        for (rel, size), cstart in zip(chunks, starts):
            vj = v_ref[0, pl.ds(cstart, size), :]
            acc = acc + jnp.dot(p_scr[sub, :, col:col + size], vj, preferred_element_type=F32)
            col += size
        o_ref[pl.ds(q_start, tq), :] = (acc / denom).astype(BF16)

    def body(qi, carry):
        q_starts = [pl.multiple_of((qi * n_sub + sub) * tq, tq) for sub in range(n_sub)]
        starts = [scores(sub, q_starts[sub]) for sub in range(n_sub)]
        denoms = [softmax(sub) for sub in range(n_sub)]
        for sub in range(n_sub):
            weighted_values(sub, q_starts[sub], starts[sub], denoms[sub])
        return carry

    lax.fori_loop(0, seq_len // (n_sub * tq), body, 0)


def _window_attn(qk, v, sink, bias, *, n_q_heads, group, chunks, tq, seq_len, n_seq, row_block0):
    has_sink = sink is not None
    if not has_sink:
        sink = jnp.zeros((n_q_heads,), F32)
    span = bias.shape[1]
    n_sub = 2
    assert seq_len % (n_sub * tq) == 0
    seq_spec = lambda head_of: pl.BlockSpec((1, seq_len, HEAD_DIM),
                                            lambda b, h: (head_of(h), row_block0 + b, 0))
    return pl.pallas_call(
        functools.partial(_window_attn_kernel, seq_len=seq_len, tq=tq, chunks=chunks, has_sink=has_sink),
        grid=(n_seq, n_q_heads),
        in_specs=[pl.BlockSpec(memory_space=pltpu.SMEM),
                  seq_spec(lambda h: h),
                  seq_spec(lambda h: n_q_heads + h // group),
                  seq_spec(lambda h: h // group),
                  pl.BlockSpec((tq, span), lambda b, h: (0, 0))],
        out_specs=pl.BlockSpec((seq_len, HEAD_DIM), lambda b, h: (b, h)),
        out_shape=jax.ShapeDtypeStruct((n_seq * seq_len, n_q_heads * HEAD_DIM), BF16),
        scratch_shapes=[pltpu.VMEM((n_sub, tq, span), F32), pltpu.VMEM((n_sub, tq, span), BF16)],
        compiler_params=_params("parallel", "parallel"),
        name="window_attn_sink" if has_sink else "window_attn_dilated",
    )(sink, qk, qk, v, bias)


def _band_bias(tq, chunks, mult_of_offset):
    rows = np.arange(tq)[:, None]
    cols = np.concatenate([rel + np.arange(size) for rel, size in chunks])[None, :]
    mult = mult_of_offset(rows - cols)
    return jnp.asarray(np.where(mult > 0, np.log2(np.maximum(mult, 1).astype(np.float64)), NEG_INF), F32)


def _mult_a(d):
    return (np.abs(d) <= A_HALF_WINDOW).astype(np.int64)


def _mult_c(d):
    mult = np.zeros_like(d)
    for window, dil in C_BRANCHES:
        reach = (window // (2 * dil)) * dil
        mult = mult + ((np.abs(d) <= reach) & (d % dil == 0))
    return mult


def _diff_attn_kernel(q_ref, k_ref, v_ref, lq1_ref, lk1_ref, lq2_ref, lk2_ref, g_ref, o_ref,
                      s_scr, p_scr, acc_scr, m_scr, l_scr, a_scr, *, seq_len, tk, lambda_init):
    tq = q_ref.shape[1]
    n_lane_blocks = tk // LANE
    rows_per_pass = min(tq, ROWS_PER_PASS)
    m_scr[...] = jnp.full(m_scr.shape, NEG_INF, F32)
    l_scr[...] = jnp.zeros(l_scr.shape, F32)
    acc_scr[...] = jnp.zeros(acc_scr.shape, F32)

    def body(kj, carry):
        k_start = pl.multiple_of(kj * tk, tk)
        for c in range(2):
            s_scr[c] = lax.dot_general(q_ref[c], k_ref[c, pl.ds(k_start, tk), :], _NT,
                                       preferred_element_type=F32)
        for c in range(2):
            for r0 in range(0, tq, rows_per_pass):
                rows = slice(r0, r0 + rows_per_pass)
                mx = s_scr[c, rows, 0:LANE]
                for cb in range(1, n_lane_blocks):
                    mx = jnp.maximum(mx, s_scr[c, rows, cb * LANE:(cb + 1) * LANE])
                m_old = m_scr[c, rows, :]
                m_new = jnp.maximum(m_old, jnp.max(mx, axis=-1, keepdims=True))
                a_scr[c, rows, :] = jnp.exp2(m_old - m_new)
                m_scr[c, rows, :] = m_new
        v = v_ref[0, pl.ds(k_start, tk), :]
        for c in range(2):
            for r0 in range(0, tq, rows_per_pass):
                rows = slice(r0, r0 + rows_per_pass)
                m = m_scr[c, rows, :]
                lsum = a_scr[c, rows, :] * l_scr[c, rows, :]
                for cb in range(n_lane_blocks):
                    cols = slice(cb * LANE, (cb + 1) * LANE)
                    p = jnp.exp2(s_scr[c, rows, cols] - m)
                    lsum = lsum + p
                    p_scr[c, rows, cols] = p.astype(BF16)
                l_scr[c, rows, :] = lsum
            pv = jnp.dot(p_scr[c], v, preferred_element_type=F32)
            a = a_scr[c]
            for e in range(acc_scr.shape[2] // LANE):
                cols = slice(e * LANE, (e + 1) * LANE)
                acc_scr[c, :, cols] = a * acc_scr[c, :, cols] + pv[:, cols]
        return carry

    lax.fori_loop(0, seq_len // tk, body, 0)

    lam = (jnp.exp(jnp.sum(lq1_ref[...] * lk1_ref[...], keepdims=True))
           - jnp.exp(jnp.sum(lq2_ref[...] * lk2_ref[...], keepdims=True)) + lambda_init)
    o1 = acc_scr[0] / jnp.sum(l_scr[0], axis=-1, keepdims=True)
    o2 = acc_scr[1] / jnp.sum(l_scr[1], axis=-1, keepdims=True)
    o = o1 - lam * o2
    o = o * lax.rsqrt(jnp.mean(o * o, axis=-1, keepdims=True) + LN_EPS)
    o_ref[...] = (o * g_ref[...] * (1.0 - lambda_init)).astype(BF16)


def _diff_attn(qk, v, lq1, lk1, lq2, lk2, subln_g, lambda_init, *, n_heads, seq_len, n_seq, row_block0):
    tq = _tile(seq_len, (512, 256, 128))
    tk = _tile(seq_len, (1024, 512, 256, 128))
    q_blocks = seq_len // tq
    vd = 2 * HEAD_DIM
    vec = lambda width: pl.BlockSpec((1, width), lambda b, h, i: (0, 0))
    return pl.pallas_call(
        functools.partial(_diff_attn_kernel, seq_len=seq_len, tk=tk, lambda_init=lambda_init),
        grid=(n_seq, n_heads, q_blocks),
        in_specs=[pl.BlockSpec((2, tq, HEAD_DIM), lambda b, h, i: (h, (row_block0 + b) * q_blocks + i, 0)),
                  pl.BlockSpec((2, seq_len, HEAD_DIM), lambda b, h, i: (n_heads + h, row_block0 + b, 0)),
                  pl.BlockSpec((1, seq_len, vd), lambda b, h, i: (h, row_block0 + b, 0)),
                  vec(HEAD_DIM), vec(HEAD_DIM), vec(HEAD_DIM), vec(HEAD_DIM), vec(vd)],
        out_specs=pl.BlockSpec((tq, vd), lambda b, h, i: (b * q_blocks + i, h)),
        out_shape=jax.ShapeDtypeStruct((n_seq * seq_len, n_heads * vd), BF16),
        scratch_shapes=[pltpu.VMEM((2, tq, tk), F32), pltpu.VMEM((2, tq, tk), BF16),
                        pltpu.VMEM((2, tq, vd), F32), pltpu.VMEM((2, tq, LANE), F32),
                        pltpu.VMEM((2, tq, LANE), F32), pltpu.VMEM((2, tq, LANE), F32)],
        compiler_params=_params("parallel", "parallel", "arbitrary"),
        name="diff_attn",
    )(qk, qk, v, lq1.reshape(1, -1), lk1.reshape(1, -1), lq2.reshape(1, -1), lk2.reshape(1, -1),
      subln_g.reshape(1, -1))


def _rope_tables(positions, q_scale):
    half = ROT_DIM // 2
    inv_freq = ROPE_THETA ** (-(jnp.arange(0, ROT_DIM, 2, dtype=F32) / ROT_DIM))
    ang = positions[:, None] * inv_freq[None, :]
    cos, sin = jnp.cos(ang), jnp.sin(ang)
    m = positions.shape[0]
    zero_half = jnp.zeros((m, half), F32)
    rest = HEAD_DIM - ROT_DIM
    c = jnp.concatenate([cos, cos, jnp.ones((m, rest), F32)], axis=1)
    s1 = jnp.concatenate([-sin, zero_half, jnp.zeros((m, rest), F32)], axis=1)
    s2 = jnp.concatenate([zero_half, sin, jnp.zeros((m, rest), F32)], axis=1)
    tab = jnp.stack([c, s1, s2])
    return jnp.stack([tab * q_scale, tab])


def _seq_groups(seq_lens_and_counts):
    groups, row = [], 0
    for seq_len, n_seq in seq_lens_and_counts:
        assert row % seq_len == 0
        groups.append((seq_len, n_seq, row // seq_len))
        row += seq_len * n_seq
    return groups


def kernel(x_prompt, x_sample, a_w_qkv, a_w_o, a_sink, b_w_qkv, b_w_o, b_lambda_q1, b_lambda_k1,
           b_lambda_q2, b_lambda_k2, b_subln_g, c_w_qkv, c_w_o, ln1_g, ln1_b, ln2_g, ln2_b,
           w_gate, w_up, w_down):
    d = x_prompt.shape[-1]
    depth = ln1_g.shape[0]
    n_heads = d // HEAD_DIM
    n_kv_a = n_heads // A_GROUP
    b_heads = n_heads // 2
    d_ff = w_gate.shape[-1]
    alpha = (2.0 * depth) ** 0.25
    q_scale = HEAD_DIM ** -0.5 * LOG2_E

    groups = _seq_groups([(x_prompt.shape[1], x_prompt.shape[0]), (x_sample.shape[1], x_sample.shape[0])])
    x = jnp.concatenate([x_prompt.reshape(-1, d), x_sample.reshape(-1, d)], axis=0)
    xb = x.astype(BF16)
    positions = jnp.concatenate([jnp.tile(jnp.arange(seq_len, dtype=F32), n_seq)
                                 for seq_len, n_seq, _ in groups])
    tab = _rope_tables(positions, q_scale)

    tq_a, tq_c = 256, 256
    chunks_a = ((-A_HALF_WINDOW, A_HALF_WINDOW), (0, tq_a), (tq_a, A_HALF_WINDOW))
    reach_c = max((w // (2 * dil)) * dil for w, dil in C_BRANCHES)
    chunks_c = tuple((rel, tq_c) for rel in range(-reach_c, tq_c + reach_c, tq_c))
    bias_a = _band_bias(tq_a, chunks_a, _mult_a)
    bias_c = _band_bias(tq_c, chunks_c, _mult_c)

    for i in range(depth):
        kind, j = i % N_MIXERS, i // N_MIXERS
        if kind == 0:
            nq = n_heads * HEAD_DIM
            nk = n_kv_a * HEAD_DIM
            w_qkv = a_w_qkv[j]
            qk = _proj_heads(xb, w_qkv[:, :nq + nk].astype(BF16), tab, n_q_cols=nq)
            v = _proj_heads(xb, w_qkv[:, nq + nk:].astype(BF16))
            attn = [_window_attn(qk, v, a_sink[j], bias_a, n_q_heads=n_heads, group=A_GROUP,
                                 chunks=chunks_a, tq=tq_a, seq_len=sl, n_seq=ns, row_block0=rb)
                    for sl, ns, rb in groups]
            w_o = a_w_o[j]
        elif kind == 1:
            lambda_init = 0.8 - 0.6 * math.exp(-0.3 * i)
            w_qkv = b_w_qkv[j]
            qk = _proj_heads(xb, w_qkv[:, :2 * d].astype(BF16), tab, n_q_cols=d)
            v = _proj_heads(xb, w_qkv[:, 2 * d:].astype(BF16), chunk=2 * HEAD_DIM)
            attn = [_diff_attn(qk, v, b_lambda_q1[j], b_lambda_k1[j], b_lambda_q2[j], b_lambda_k2[j],
                               b_subln_g[j], lambda_init, n_heads=b_heads, seq_len=sl, n_seq=ns,
                               row_block0=rb)
                    for sl, ns, rb in groups]
            w_o = b_w_o[j]
        else:
            w_qkv = c_w_qkv[j]
            qk = _proj_heads(xb, w_qkv[:, :2 * d].astype(BF16), tab, n_q_cols=d)
            v = _proj_heads(xb, w_qkv[:, 2 * d:].astype(BF16))
            attn = [_window_attn(qk, v, None, bias_c, n_q_heads=n_heads, group=1,
                                 chunks=chunks_c, tq=tq_c, seq_len=sl, n_seq=ns, row_block0=rb)
                    for sl, ns, rb in groups]
            w_o = c_w_o[j]
        h = _matmul(jnp.concatenate(attn, axis=0), w_o.astype(BF16))
        x, xb = _deepnorm_ln(x, h, ln1_g[i], ln1_b[i], alpha)

        hidden = _gateup(xb, w_gate[i].astype(BF16), w_up[i].astype(BF16))
        h = _matmul(hidden, w_down[i].astype(BF16))
        x, xb = _deepnorm_ln(x, h, ln2_g[i], ln2_b[i], alpha)

    n_prompt = x_prompt.shape[0] * x_prompt.shape[1]
    return x[:n_prompt].reshape(x_prompt.shape), x[n_prompt:].reshape(x_sample.shape)
```

```python
import functools
import math

import numpy as np
import jax
import jax.numpy as jnp
from jax import lax
from jax.experimental import pallas as pl
from jax.experimental.pallas import tpu as pltpu

F32 = jnp.float32
BF16 = jnp.bfloat16

LANE = 128
HEAD_DIM = 128
LOG2_E = math.log2(math.e)
ROT_DIM = HEAD_DIM // 4
ROPE_THETA = 500000.0
LN_EPS = 1e-5
NEG_INF = -1e30
N_MIXERS = 3
A_GROUP = 4
A_HALF_WINDOW = 128
C_BRANCHES = ((128, 1), (512, 4), (2048, 16))
VMEM_LIMIT_BYTES = 56 * 1024 * 1024

ROWS_PER_PASS = 256
_NT = (((1,), (1,)), ((), ()))


def _params(*semantics):
    return pltpu.CompilerParams(dimension_semantics=semantics, vmem_limit_bytes=VMEM_LIMIT_BYTES)


def _tile(n, prefs):
    for t in prefs:
        if n % t == 0:
            return t
    raise ValueError(f"no tile in {prefs} divides {n}")


def _proj_rope_kernel(x_ref, w_ref, t_ref, o_ref):
    acc = jnp.dot(x_ref[...], w_ref[...], preferred_element_type=F32)
    cos, s1, s2 = t_ref[0, 0], t_ref[0, 1], t_ref[0, 2]
    half = ROT_DIM // 2
    for c in range(o_ref.shape[0]):
        blk = acc[:, c * HEAD_DIM:(c + 1) * HEAD_DIM]
        blk = (blk * cos + pltpu.roll(blk, HEAD_DIM - half, 1) * s1 + pltpu.roll(blk, half, 1) * s2)
        o_ref[c] = blk.astype(BF16)


def _proj_plain_kernel(x_ref, w_ref, o_ref):
    acc = jnp.dot(x_ref[...], w_ref[...], preferred_element_type=F32)
    chunk = o_ref.shape[2]
    for c in range(o_ref.shape[0]):
        o_ref[c] = acc[:, c * chunk:(c + 1) * chunk].astype(BF16)


def _proj_heads(xb, w, col0, n, tab=None, n_q_cols=0, groups=(), chunk=HEAD_DIM):
    m, k = xb.shape
    tm = _tile(math.gcd(m, *(sl for sl, _, _ in groups)), (1024, 512, 256))
    tn = _tile(math.gcd(n, n_q_cols, col0), (1024, 512, 256, 128))
    grid = (m // tm, n // tn)
    col_tile0 = col0 // tn
    in_specs = [pl.BlockSpec((tm, k), lambda i, j: (i, 0)),
                pl.BlockSpec((k, tn), lambda i, j: (0, col_tile0 + j))]
    args = [xb, w]
    if tab is not None:
        assert chunk == HEAD_DIM
        n_q_tiles = n_q_cols // tn

        def position_tile(i):
            pos, tile0 = i, 0
            for seq_len, n_seq, _ in groups:
                pos = jnp.where(i >= tile0, (i - tile0) % (seq_len // tm), pos)
                tile0 += seq_len * n_seq // tm
            return pos

        in_specs.append(pl.BlockSpec((1, 3, tm, HEAD_DIM),
                                     lambda i, j: (jnp.where(j < n_q_tiles, 0, 1), 0, position_tile(i), 0)))
        args.append(tab)
        body = _proj_rope_kernel
    else:
        body = _proj_plain_kernel
    return pl.pallas_call(
        body,
        grid=grid,
        in_specs=in_specs,
        out_specs=pl.BlockSpec((tn // chunk, tm, chunk), lambda i, j: (j, i, 0)),
        out_shape=jax.ShapeDtypeStruct((n // chunk, m, chunk), BF16),
        compiler_params=_params("parallel", "parallel"),
        name="proj_heads_rope" if tab is not None else "proj_heads",
    )(*args)


def _mm_kernel(a_ref, w_ref, o_ref):
    o_ref[...] = jnp.dot(a_ref[...], w_ref[...], preferred_element_type=F32).astype(o_ref.dtype)


def _matmul(a, w, out_dtype=F32):
    m, k = a.shape
    n = w.shape[1]
    max_tile = 1024 if 8 * 1024 * k <= VMEM_LIMIT_BYTES * 3 // 4 else 512
    tm = _tile(m, tuple(t for t in (1024, 512, 256) if t <= max_tile))
    tn = _tile(n, tuple(t for t in (1024, 512, 256, 128) if t <= max_tile))
    return pl.pallas_call(
        _mm_kernel,
        grid=(m // tm, n // tn),
        in_specs=[pl.BlockSpec((tm, k), lambda i, j: (i, 0)),
                  pl.BlockSpec((k, tn), lambda i, j: (0, j))],
        out_specs=pl.BlockSpec((tm, tn), lambda i, j: (i, j)),
        out_shape=jax.ShapeDtypeStruct((m, n), out_dtype),
        compiler_params=_params("parallel", "parallel"),
        name="matmul",
    )(a, w)


def _gateup_kernel(x_ref, wg_ref, wu_ref, o_ref):
    x = x_ref[...]
    g = jnp.dot(x, wg_ref[...], preferred_element_type=F32)
    u = jnp.dot(x, wu_ref[...], preferred_element_type=F32)
    o_ref[...] = (g * jax.nn.sigmoid(g) * u).astype(BF16)


def _gateup(xb, wg, wu):
    m, k = xb.shape
    f = wg.shape[1]
    tn = _tile(f, (512, 256, 128))
    tm = _tile(m, (2048, 1024, 512, 256) if tn <= 256 else (1024, 512, 256))
    return pl.pallas_call(
        _gateup_kernel,
        grid=(m // tm, f // tn),
        in_specs=[pl.BlockSpec((tm, k), lambda i, j: (i, 0)),
                  pl.BlockSpec((k, tn), lambda i, j: (0, j)),
                  pl.BlockSpec((k, tn), lambda i, j: (0, j))],
        out_specs=pl.BlockSpec((tm, tn), lambda i, j: (i, j)),
        out_shape=jax.ShapeDtypeStruct((m, f), BF16),
        compiler_params=_params("parallel", "parallel"),
        name="ffn_gate_up",
    )(xb, wg, wu)


def _ln_kernel(x_ref, h_ref, g_ref, b_ref, xo_ref, *maybe_xb_ref, alpha):
    y = alpha * x_ref[...] + h_ref[...]
    mu = jnp.mean(y, axis=-1, keepdims=True)
    yc = y - mu
    var = jnp.mean(yc * yc, axis=-1, keepdims=True)
    out = yc * lax.rsqrt(var + LN_EPS) * g_ref[...] + b_ref[...]
    xo_ref[...] = out
    for xb_ref in maybe_xb_ref:
        xb_ref[...] = out.astype(BF16)


def _deepnorm_ln(x, h, g, b, alpha, rows=None):
    m, d = x.shape
    row0, n_rows = rows or (0, m)
    tm = _tile(math.gcd(row0, n_rows), (256, 128))
    tile0 = row0 // tm
    row_in = pl.BlockSpec((tm, d), lambda i: (tile0 + i, 0))
    row_out = pl.BlockSpec((tm, d), lambda i: (i, 0))
    vec = pl.BlockSpec((1, d), lambda i: (0, 0))
    out_dtypes = (F32,) if rows else (F32, BF16)
    outs = pl.pallas_call(
        functools.partial(_ln_kernel, alpha=alpha),
        grid=(n_rows // tm,),
        in_specs=[row_in, row_in, vec, vec],
        out_specs=[row_out] * len(out_dtypes),
        out_shape=[jax.ShapeDtypeStruct((n_rows, d), dt) for dt in out_dtypes],
        compiler_params=_params("parallel"),
        name="deepnorm_ln",
    )(x, h, g.reshape(1, d), b.reshape(1, d))
    return outs[0] if rows else outs


def _into(out, in_specs, args):
    if out is None:
        return dict(in_specs=in_specs)
    args.append(out)
    return dict(in_specs=in_specs + [pl.BlockSpec(memory_space=pl.ANY)],
                input_output_aliases={len(args) - 1: 0})


def _window_attn_kernel(sink_ref, q_ref, k_ref, v_ref, bias_ref, *maybe_out_alias_then_out_and_scratch,
                        seq_len, tq, chunks, has_sink):
    o_ref, s_scr, p_scr = maybe_out_alias_then_out_and_scratch[-3:]
    head = pl.program_id(1)
    n_sub = s_scr.shape[0]
    n_lane_blocks = s_scr.shape[2] // LANE

    def scores(sub, q_start):
        q = q_ref[0, pl.ds(q_start, tq), :]
        starts = []
        col = 0
        for rel, size in chunks:
            start = q_start + rel
            inside = jnp.logical_and(start >= 0, start + size <= seq_len)
            cstart = pl.multiple_of(jnp.clip(start, 0, seq_len - size), HEAD_DIM)
            starts.append(cstart)
            kj = k_ref[0, pl.ds(cstart, size), :]
            sj = lax.dot_general(q, kj, _NT, preferred_element_type=F32)
            pen = jnp.where(inside, 0.0, NEG_INF).astype(F32)
            s_scr[sub, :, col:col + size] = sj + bias_ref[:, col:col + size] + pen
            col += size
        return starts

    def softmax(sub):
        m = s_scr[sub, :, 0:LANE]
        for cb in range(1, n_lane_blocks):
            m = jnp.maximum(m, s_scr[sub, :, cb * LANE:(cb + 1) * LANE])
        m = jnp.broadcast_to(jnp.max(m, axis=-1, keepdims=True), (tq, LANE))
        if has_sink:
            sk = sink_ref[head] * LOG2_E
            m = jnp.maximum(m, sk)
        lsum = jnp.zeros((tq, LANE), F32)
        for cb in range(n_lane_blocks):
            cols = slice(cb * LANE, (cb + 1) * LANE)
            p = jnp.exp2(s_scr[sub, :, cols] - m)
            lsum = lsum + p
            p_scr[sub, :, cols] = p.astype(BF16)
        denom = jnp.sum(lsum, axis=-1, keepdims=True)
        if has_sink:
            denom = denom + jnp.exp2(sk - m[:, 0:1])
        return denom

    def weighted_values(sub, q_start, starts, denom):
        acc = jnp.zeros((tq, HEAD_DIM), F32)
        col = 0
        for (rel, size), cstart in zip(chunks, starts):
            vj = v_ref[0, pl.ds(cstart, size), :]
            acc = acc + jnp.dot(p_scr[sub, :, col:col + size], vj, preferred_element_type=F32)
            col += size
        o_ref[pl.ds(q_start, tq), :] = (acc / denom).astype(BF16)

    def body(qi, carry):
        q_starts = [pl.multiple_of((qi * n_sub + sub) * tq, tq) for sub in range(n_sub)]
        starts = [scores(sub, q_starts[sub]) for sub in range(n_sub)]
        denoms = [softmax(sub) for sub in range(n_sub)]
        for sub in range(n_sub):
            weighted_values(sub, q_starts[sub], starts[sub], denoms[sub])
        return carry

    lax.fori_loop(0, seq_len // (n_sub * tq), body, 0)


def _window_attn(qk, v, sink, mult_of_offset, reach, out, *, n_q_heads, group, tq, n_sub, seq_len, n_seq,
                 row_block0, name):
    has_sink = sink is not None
    if not has_sink:
        sink = jnp.zeros((n_q_heads,), F32)
    piece = min(reach, tq)
    assert reach % piece == 0 and tq % piece == 0 and seq_len % (n_sub * tq) == 0
    side = reach // piece
    chunks = (tuple((-reach + i * piece, piece) for i in range(side)) + ((0, tq),)
              + tuple((tq + i * piece, piece) for i in range(side)))
    bias = _band_bias(tq, chunks, mult_of_offset)
    span = bias.shape[1]
    seq_spec = lambda head_of: pl.BlockSpec((1, seq_len, HEAD_DIM),
                                            lambda b, h: (head_of(h), row_block0 + b, 0))
    in_specs = [pl.BlockSpec(memory_space=pltpu.SMEM),
                seq_spec(lambda h: h),
                seq_spec(lambda h: n_q_heads + h // group),
                seq_spec(lambda h: h // group),
                pl.BlockSpec((tq, span), lambda b, h: (0, 0), pipeline_mode=pl.Buffered(1))]
    args = [sink, qk, qk, v, bias]
    return pl.pallas_call(
        functools.partial(_window_attn_kernel, seq_len=seq_len, tq=tq, chunks=chunks, has_sink=has_sink),
        grid=(n_seq, n_q_heads),
        out_specs=pl.BlockSpec((seq_len, HEAD_DIM), lambda b, h: (row_block0 + b, h)),
        out_shape=jax.ShapeDtypeStruct((qk.shape[1], n_q_heads * HEAD_DIM), BF16),
        scratch_shapes=[pltpu.VMEM((n_sub, tq, span), F32), pltpu.VMEM((n_sub, tq, span), BF16)],
        compiler_params=_params("parallel", "parallel"),
        name=name,
        **_into(out, in_specs, args),
    )(*args)


def _band_bias(tq, chunks, mult_of_offset):
    rows = np.arange(tq)[:, None]
    cols = np.concatenate([rel + np.arange(size) for rel, size in chunks])[None, :]
    mult = mult_of_offset(rows - cols)
    return jnp.asarray(np.where(mult > 0, np.log2(np.maximum(mult, 1).astype(np.float64)), NEG_INF), F32)


def _mult_a(d):
    return (np.abs(d) <= A_HALF_WINDOW).astype(np.int64)


def _mult_c(d):
    mult = np.zeros_like(d)
    for window, dil in C_BRANCHES:
        reach = (window // (2 * dil)) * dil
        mult = mult + ((np.abs(d) <= reach) & (d % dil == 0))
    return mult


def _diff_attn_kernel(q_ref, k_ref, v_ref, lq1_ref, lk1_ref, lq2_ref, lk2_ref, g_ref,
                      *maybe_out_alias_then_out_and_scratch, seq_len, tk, lambda_init):
    o_ref, s_scr, p_scr, acc_scr, m_scr, l_scr, a_scr = maybe_out_alias_then_out_and_scratch[-7:]
    tq = q_ref.shape[1]
    n_lane_blocks = tk // LANE
    rows_per_pass = min(tq, ROWS_PER_PASS)
    m_scr[...] = jnp.full(m_scr.shape, NEG_INF, F32)
    l_scr[...] = jnp.zeros(l_scr.shape, F32)
    acc_scr[...] = jnp.zeros(acc_scr.shape, F32)

    def scores(kj):
        k_start = pl.multiple_of(kj * tk, tk)
        for c in range(2):
            s_scr[c] = lax.dot_general(q_ref[c], k_ref[c, pl.ds(k_start, tk), :], _NT,
                                       preferred_element_type=F32)

    def row_max():
        for c in range(2):
            for r0 in range(0, tq, rows_per_pass):
                rows = slice(r0, r0 + rows_per_pass)
                mx = s_scr[c, rows, 0:LANE]
                for cb in range(1, n_lane_blocks):
                    mx = jnp.maximum(mx, s_scr[c, rows, cb * LANE:(cb + 1) * LANE])
                m_old = m_scr[c, rows, :]
                m_new = jnp.maximum(m_old, jnp.max(mx, axis=-1, keepdims=True))
                a_scr[c, rows, :] = jnp.exp2(m_old - m_new)
                m_scr[c, rows, :] = m_new

    def probs_and_values(kj):
        k_start = pl.multiple_of(kj * tk, tk)
        v = v_ref[0, pl.ds(k_start, tk), :]
        for c in range(2):
            for r0 in range(0, tq, rows_per_pass):
                rows = slice(r0, r0 + rows_per_pass)
                m = m_scr[c, rows, :]
                lsum = a_scr[c, rows, :] * l_scr[c, rows, :]
                for cb in range(n_lane_blocks):
                    cols = slice(cb * LANE, (cb + 1) * LANE)
                    p = jnp.exp2(s_scr[c, rows, cols] - m)
                    lsum = lsum + p
                    p_scr[c, rows, cols] = p.astype(BF16)
                l_scr[c, rows, :] = lsum
            pv = jnp.dot(p_scr[c], v, preferred_element_type=F32)
            a = a_scr[c]
            for e in range(acc_scr.shape[2] // LANE):
                cols = slice(e * LANE, (e + 1) * LANE)
                acc_scr[c, :, cols] = a * acc_scr[c, :, cols] + pv[:, cols]

    def body(kj, carry):
        scores(kj)
        row_max()
        probs_and_values(kj)
        return carry

    lax.fori_loop(0, seq_len // tk, body, 0)

    lam = (jnp.exp(jnp.sum(lq1_ref[...] * lk1_ref[...], keepdims=True))
           - jnp.exp(jnp.sum(lq2_ref[...] * lk2_ref[...], keepdims=True)) + lambda_init)
    o1 = acc_scr[0] / jnp.sum(l_scr[0], axis=-1, keepdims=True)
    o2 = acc_scr[1] / jnp.sum(l_scr[1], axis=-1, keepdims=True)
    o = o1 - lam * o2
    o = o * lax.rsqrt(jnp.mean(o * o, axis=-1, keepdims=True) + LN_EPS)
    o_ref[...] = (o * g_ref[...] * (1.0 - lambda_init)).astype(BF16)


def _diff_attn(qk, v, lq1, lk1, lq2, lk2, subln_g, lambda_init, out, *, n_heads, seq_len, n_seq,
               row_block0, tq=1024, tk=2048):
    tq = _tile(seq_len, tuple(t for t in (1024, 512, 256, 128) if t <= tq))
    tk = _tile(seq_len, tuple(t for t in (2048, 1024, 512, 256, 128) if t <= tk))
    q_blocks = seq_len // tq
    vd = 2 * HEAD_DIM
    vec = lambda width: pl.BlockSpec((1, width), lambda b, h, i: (0, 0))
    q_row_block = lambda b, i: (row_block0 + b) * q_blocks + i
    in_specs = [pl.BlockSpec((2, tq, HEAD_DIM), lambda b, h, i: (h, q_row_block(b, i), 0)),
                pl.BlockSpec((2, seq_len, HEAD_DIM), lambda b, h, i: (n_heads + h, row_block0 + b, 0),
                             pipeline_mode=pl.Buffered(1)),
                pl.BlockSpec((1, seq_len, vd), lambda b, h, i: (h, row_block0 + b, 0),
                             pipeline_mode=pl.Buffered(1)),
                vec(HEAD_DIM), vec(HEAD_DIM), vec(HEAD_DIM), vec(HEAD_DIM), vec(vd)]
    args = [qk, qk, v, lq1.reshape(1, -1), lk1.reshape(1, -1), lq2.reshape(1, -1), lk2.reshape(1, -1),
            subln_g.reshape(1, -1)]
    return pl.pallas_call(
        functools.partial(_diff_attn_kernel, seq_len=seq_len, tk=tk, lambda_init=lambda_init),
        grid=(n_seq, n_heads, q_blocks),
        out_specs=pl.BlockSpec((tq, vd), lambda b, h, i: (q_row_block(b, i), h)),
        out_shape=jax.ShapeDtypeStruct((qk.shape[1], n_heads * vd), BF16),
        scratch_shapes=[pltpu.VMEM((2, tq, tk), F32), pltpu.VMEM((2, tq, tk), BF16),
                        pltpu.VMEM((2, tq, vd), F32), pltpu.VMEM((2, tq, LANE), F32),
                        pltpu.VMEM((2, tq, LANE), F32), pltpu.VMEM((2, tq, LANE), F32)],
        compiler_params=_params("parallel", "parallel", "arbitrary"),
        name="diff_attn",
        **_into(out, in_specs, args),
    )(*args)


def _rope_tables(max_len, q_scale):
    half = ROT_DIM // 2
    inv_freq = ROPE_THETA ** (-(jnp.arange(0, ROT_DIM, 2, dtype=F32) / ROT_DIM))
    lane = jnp.arange(HEAD_DIM)
    inv_freq_of_lane = jnp.where(lane < ROT_DIM, inv_freq[lane % half], 0.0)
    ang = jnp.arange(max_len, dtype=F32)[:, None] * inv_freq_of_lane[None, :]
    cos, sin = jnp.cos(ang), jnp.sin(ang)
    s1 = jnp.where(lane < half, -sin, 0.0)
    s2 = jnp.where((lane >= half) & (lane < ROT_DIM), sin, 0.0)
    tab = jnp.stack([cos, s1, s2])
    return jnp.stack([tab * q_scale, tab])


def _seq_groups(seq_lens_and_counts):
    groups, row = [], 0
    for seq_len, n_seq in seq_lens_and_counts:
        assert row % seq_len == 0
        groups.append((seq_len, n_seq, row // seq_len))
        row += seq_len * n_seq
    return groups


def kernel(x_prompt, x_sample, a_w_qkv, a_w_o, a_sink, b_w_qkv, b_w_o, b_lambda_q1, b_lambda_k1,
           b_lambda_q2, b_lambda_k2, b_subln_g, c_w_qkv, c_w_o, ln1_g, ln1_b, ln2_g, ln2_b,
           w_gate, w_up, w_down):
    d = x_prompt.shape[-1]
    depth = ln1_g.shape[0]
    n_heads = d // HEAD_DIM
    n_kv_a = n_heads // A_GROUP
    b_heads = n_heads // 2
    alpha = (2.0 * depth) ** 0.25
    q_scale = HEAD_DIM ** -0.5 * LOG2_E

    groups = _seq_groups([(x_prompt.shape[1], x_prompt.shape[0]), (x_sample.shape[1], x_sample.shape[0])])
    x = jnp.concatenate([x_prompt.reshape(-1, d), x_sample.reshape(-1, d)], axis=0)
    xb = x.astype(BF16)
    tab = _rope_tables(max(seq_len for seq_len, _, _ in groups), q_scale)

    reach_c = max((w // (2 * dil)) * dil for w, dil in C_BRANCHES)

    for i in range(depth):
        kind, j = i % N_MIXERS, i // N_MIXERS
        attn = None
        if kind == 0:
            nq = n_heads * HEAD_DIM
            nk = n_kv_a * HEAD_DIM
            w_qkv = a_w_qkv[j].astype(BF16)
            qk = _proj_heads(xb, w_qkv, 0, nq + nk, tab, n_q_cols=nq, groups=groups)
            v = _proj_heads(xb, w_qkv, nq + nk, nk)
            for sl, ns, rb in groups:
                attn = _window_attn(qk, v, a_sink[j], _mult_a, A_HALF_WINDOW, attn, n_q_heads=n_heads,
                                    group=A_GROUP, tq=256, n_sub=2, seq_len=sl, n_seq=ns, row_block0=rb,
                                    name="window_attn_sink")
            w_o = a_w_o[j]
        elif kind == 1:
            lambda_init = 0.8 - 0.6 * math.exp(-0.3 * i)
            w_qkv = b_w_qkv[j].astype(BF16)
            qk = _proj_heads(xb, w_qkv, 0, 2 * d, tab, n_q_cols=d, groups=groups)
            v = _proj_heads(xb, w_qkv, 2 * d, d, chunk=2 * HEAD_DIM)
            for sl, ns, rb in groups:
                attn = _diff_attn(qk, v, b_lambda_q1[j], b_lambda_k1[j], b_lambda_q2[j], b_lambda_k2[j],
                                  b_subln_g[j], lambda_init, attn, n_heads=b_heads, seq_len=sl, n_seq=ns,
                                  row_block0=rb)
            w_o = b_w_o[j]
        else:
            w_qkv = c_w_qkv[j].astype(BF16)
            qk = _proj_heads(xb, w_qkv, 0, 2 * d, tab, n_q_cols=d, groups=groups)
            v = _proj_heads(xb, w_qkv, 2 * d, d)
            for sl, ns, rb in groups:
                attn = _window_attn(qk, v, None, _mult_c, reach_c, attn, n_q_heads=n_heads, group=1,
                                    tq=256, n_sub=2, seq_len=sl, n_seq=ns, row_block0=rb,
                                    name="window_attn_dilated")
            w_o = c_w_o[j]
        h = _matmul(attn, w_o.astype(BF16))
        x, xb = _deepnorm_ln(x, h, ln1_g[i], ln1_b[i], alpha)

        hidden = _gateup(xb, w_gate[i].astype(BF16), w_up[i].astype(BF16))
        h = _matmul(hidden, w_down[i].astype(BF16))
        if i + 1 < depth:
            x, xb = _deepnorm_ln(x, h, ln2_g[i], ln2_b[i], alpha)

    n_prompt = x_prompt.shape[0] * x_prompt.shape[1]
    y_prompt = _deepnorm_ln(x, h, ln2_g[-1], ln2_b[-1], alpha, rows=(0, n_prompt))
    y_sample = _deepnorm_ln(x, h, ln2_g[-1], ln2_b[-1], alpha, rows=(n_prompt, x.shape[0] - n_prompt))
    return y_prompt.reshape(x_prompt.shape), y_sample.reshape(x_sample.shape)
```

```python
import functools
import math

import numpy as np
import jax
import jax.numpy as jnp
from jax import lax
from jax.experimental import pallas as pl
from jax.experimental.pallas import tpu as pltpu

F32 = jnp.float32
BF16 = jnp.bfloat16

LANE = 128
HEAD_DIM = 128
LOG2_E = math.log2(math.e)
ROT_DIM = HEAD_DIM // 4
ROPE_THETA = 500000.0
LN_EPS = 1e-5
NEG_INF = -1e30
N_MIXERS = 3
A_GROUP = 4
A_HALF_WINDOW = 128
C_BRANCHES = ((128, 1), (512, 4), (2048, 16))
VMEM_LIMIT_BYTES = 56 * 1024 * 1024

_NT = (((1,), (1,)), ((), ()))


def _params(*semantics):
    return pltpu.CompilerParams(dimension_semantics=semantics, vmem_limit_bytes=VMEM_LIMIT_BYTES)


def _tile(n, prefs):
    for t in prefs:
        if n % t == 0:
            return t
    raise ValueError(f"no tile in {prefs} divides {n}")


def _proj_rope_kernel(x_ref, w_ref, t_ref, o_ref):
    cos, s1, s2 = t_ref[0, 0], t_ref[0, 1], t_ref[0, 2]
    half = ROT_DIM // 2
    n_heads = o_ref.shape[0]
    heads_per_dot = max(1, n_heads // 2)
    x = x_ref[...]
    for c0 in range(0, n_heads, heads_per_dot):
        acc = jnp.dot(x, w_ref[:, c0 * HEAD_DIM:(c0 + heads_per_dot) * HEAD_DIM], preferred_element_type=F32)
        for c in range(heads_per_dot):
            blk = acc[:, c * HEAD_DIM:(c + 1) * HEAD_DIM]
            blk = (blk * cos + pltpu.roll(blk, HEAD_DIM - half, 1) * s1 + pltpu.roll(blk, half, 1) * s2)
            o_ref[c0 + c] = blk.astype(BF16)


def _proj_plain_kernel(x_ref, w_ref, o_ref):
    acc = jnp.dot(x_ref[...], w_ref[...], preferred_element_type=F32)
    chunk = o_ref.shape[2]
    for c in range(o_ref.shape[0]):
        o_ref[c] = acc[:, c * chunk:(c + 1) * chunk].astype(BF16)


def _proj_heads(xb, w, col0, n, tab=None, n_q_cols=0, groups=(), chunk=HEAD_DIM):
    m, k = xb.shape
    tm = _tile(math.gcd(m, *(sl for sl, _, _ in groups)), (1024, 512, 256))
    tn = _tile(math.gcd(n, n_q_cols, col0), (1024, 512, 256, 128))
    grid = (m // tm, n // tn)
    col_tile0 = col0 // tn
    in_specs = [pl.BlockSpec((tm, k), lambda i, j: (i, 0)),
                pl.BlockSpec((k, tn), lambda i, j: (0, col_tile0 + j))]
    args = [xb, w]
    if tab is not None:
        assert chunk == HEAD_DIM
        n_q_tiles = n_q_cols // tn

        def position_tile(i):
            pos, tile0 = i, 0
            for seq_len, n_seq, _ in groups:
                pos = jnp.where(i >= tile0, (i - tile0) % (seq_len // tm), pos)
                tile0 += seq_len * n_seq // tm
            return pos

        in_specs.append(pl.BlockSpec((1, 3, tm, HEAD_DIM),
                                     lambda i, j: (jnp.where(j < n_q_tiles, 0, 1), 0, position_tile(i), 0)))
        args.append(tab)
        body = _proj_rope_kernel
    else:
        body = _proj_plain_kernel
    return pl.pallas_call(
        body,
        grid=grid,
        in_specs=in_specs,
        out_specs=pl.BlockSpec((tn // chunk, tm, chunk), lambda i, j: (j, i, 0)),
        out_shape=jax.ShapeDtypeStruct((n // chunk, m, chunk), BF16),
        compiler_params=_params("parallel", "parallel"),
        name="proj_heads_rope" if tab is not None else "proj_heads",
    )(*args)


def _mm_kernel(a_ref, w_ref, o_ref):
    o_ref[...] = jnp.dot(a_ref[...], w_ref[...], preferred_element_type=F32).astype(o_ref.dtype)


def _matmul(a, w, out_dtype=F32):
    m, k = a.shape
    n = w.shape[1]
    max_tile = 1024 if 8 * 1024 * k <= VMEM_LIMIT_BYTES * 3 // 4 else 512
    tm = _tile(m, tuple(t for t in (1024, 512, 256) if t <= max_tile))
    tn = _tile(n, tuple(t for t in (1024, 512, 256, 128) if t <= max_tile))
    return pl.pallas_call(
        _mm_kernel,
        grid=(m // tm, n // tn),
        in_specs=[pl.BlockSpec((tm, k), lambda i, j: (i, 0)),
                  pl.BlockSpec((k, tn), lambda i, j: (0, j))],
        out_specs=pl.BlockSpec((tm, tn), lambda i, j: (i, j)),
        out_shape=jax.ShapeDtypeStruct((m, n), out_dtype),
        compiler_params=_params("parallel", "parallel"),
        name="matmul",
    )(a, w)


def _gateup_kernel(x_ref, wg_ref, wu_ref, o_ref):
    x = x_ref[...]
    g = jnp.dot(x, wg_ref[...], preferred_element_type=F32)
    u = jnp.dot(x, wu_ref[...], preferred_element_type=F32)
    o_ref[...] = (g * jax.nn.sigmoid(g) * u).astype(BF16)


def _gateup(xb, wg, wu):
    m, k = xb.shape
    f = wg.shape[1]
    tn = _tile(f, (512, 256, 128))
    tm = _tile(m, (2048, 1024, 512, 256) if tn <= 256 else (1024, 512, 256))
    return pl.pallas_call(
        _gateup_kernel,
        grid=(m // tm, f // tn),
        in_specs=[pl.BlockSpec((tm, k), lambda i, j: (i, 0)),
                  pl.BlockSpec((k, tn), lambda i, j: (0, j)),
                  pl.BlockSpec((k, tn), lambda i, j: (0, j))],
        out_specs=pl.BlockSpec((tm, tn), lambda i, j: (i, j)),
        out_shape=jax.ShapeDtypeStruct((m, f), BF16),
        compiler_params=_params("parallel", "parallel"),
        name="ffn_gate_up",
    )(xb, wg, wu)


def _ln_kernel(x_ref, h_ref, g_ref, b_ref, xo_ref, *maybe_xb_ref, alpha):
    y = alpha * x_ref[...] + h_ref[...]
    mu = jnp.mean(y, axis=-1, keepdims=True)
    yc = y - mu
    var = jnp.mean(yc * yc, axis=-1, keepdims=True)
    out = yc * lax.rsqrt(var + LN_EPS) * g_ref[...] + b_ref[...]
    xo_ref[...] = out
    for xb_ref in maybe_xb_ref:
        xb_ref[...] = out.astype(BF16)


def _deepnorm_ln(x, h, g, b, alpha, rows=None):
    m, d = x.shape
    row0, n_rows = rows or (0, m)
    tm = _tile(math.gcd(row0, n_rows), (256, 128))
    tile0 = row0 // tm
    row_in = pl.BlockSpec((tm, d), lambda i: (tile0 + i, 0))
    row_out = pl.BlockSpec((tm, d), lambda i: (i, 0))
    vec = pl.BlockSpec((1, d), lambda i: (0, 0))
    out_dtypes = (F32,) if rows else (F32, BF16)
    outs = pl.pallas_call(
        functools.partial(_ln_kernel, alpha=alpha),
        grid=(n_rows // tm,),
        in_specs=[row_in, row_in, vec, vec],
        out_specs=[row_out] * len(out_dtypes),
        out_shape=[jax.ShapeDtypeStruct((n_rows, d), dt) for dt in out_dtypes],
        compiler_params=_params("parallel"),
        name="deepnorm_ln",
    )(x, h, g.reshape(1, d), b.reshape(1, d))
    return outs[0] if rows else outs


def _into(out, in_specs, args):
    if out is None:
        return dict(in_specs=in_specs)
    args.append(out)
    return dict(in_specs=in_specs + [pl.BlockSpec(memory_space=pl.ANY)],
                input_output_aliases={len(args) - 1: 0})


def _window_attn_kernel(sink_ref, q_ref, k_ref, v_ref, bias_ref, *maybe_out_alias_then_out_and_scratch,
                        seq_len, tq, chunks, has_sink):
    o_ref, s_scr, p_scr = maybe_out_alias_then_out_and_scratch[-3:]
    head = pl.program_id(1)
    n_sub = s_scr.shape[0]
    n_lane_blocks = s_scr.shape[2] // LANE

    def scores(sub, q_start):
        q = q_ref[0, pl.ds(q_start, tq), :]
        starts = []
        col = 0
        for rel, size in chunks:
            start = q_start + rel
            inside = jnp.logical_and(start >= 0, start + size <= seq_len)
            cstart = pl.multiple_of(jnp.clip(start, 0, seq_len - size), HEAD_DIM)
            starts.append(cstart)
            kj = k_ref[0, pl.ds(cstart, size), :]
            sj = lax.dot_general(q, kj, _NT, preferred_element_type=F32)
            pen = jnp.where(inside, 0.0, NEG_INF).astype(F32)
            s_scr[sub, :, col:col + size] = sj + bias_ref[:, col:col + size] + pen
            col += size
        return starts

    def softmax(sub):
        m = s_scr[sub, :, 0:LANE]
        for cb in range(1, n_lane_blocks):
            m = jnp.maximum(m, s_scr[sub, :, cb * LANE:(cb + 1) * LANE])
        m = jnp.broadcast_to(jnp.max(m, axis=-1, keepdims=True), (tq, LANE))
        if has_sink:
            sk = sink_ref[head] * LOG2_E
            m = jnp.maximum(m, sk)
        lsum = jnp.zeros((tq, LANE), F32)
        for cb in range(n_lane_blocks):
            cols = slice(cb * LANE, (cb + 1) * LANE)
            p = jnp.exp2(s_scr[sub, :, cols] - m)
            lsum = lsum + p
            p_scr[sub, :, cols] = p.astype(BF16)
        denom = jnp.sum(lsum, axis=-1, keepdims=True)
        if has_sink:
            denom = denom + jnp.exp2(sk - m[:, 0:1])
        return denom

    def weighted_values(sub, q_start, starts, denom):
        acc = jnp.zeros((tq, HEAD_DIM), F32)
        col = 0
        for (rel, size), cstart in zip(chunks, starts):
            vj = v_ref[0, pl.ds(cstart, size), :]
            acc = acc + jnp.dot(p_scr[sub, :, col:col + size], vj, preferred_element_type=F32)
            col += size
        o_ref[pl.ds(q_start, tq), :] = (acc / denom).astype(BF16)

    def body(qi, carry):
        q_starts = [pl.multiple_of((qi * n_sub + sub) * tq, tq) for sub in range(n_sub)]
        starts = [scores(sub, q_starts[sub]) for sub in range(n_sub)]
        denoms = [softmax(sub) for sub in range(n_sub)]
        for sub in range(n_sub):
            weighted_values(sub, q_starts[sub], starts[sub], denoms[sub])
        return carry

    lax.fori_loop(0, seq_len // (n_sub * tq), body, 0)


def _window_attn(qk, v, sink, mult_of_offset, reach, out, *, n_q_heads, group, tq, n_sub, seq_len, n_seq,
                 row_block0, name):
    has_sink = sink is not None
    if not has_sink:
        sink = jnp.zeros((n_q_heads,), F32)
    piece = min(reach, tq)
    assert reach % piece == 0 and tq % piece == 0 and seq_len % (n_sub * tq) == 0
    side = reach // piece
    chunks = (tuple((-reach + i * piece, piece) for i in range(side)) + ((0, tq),)
              + tuple((tq + i * piece, piece) for i in range(side)))
    bias = _band_bias(tq, chunks, mult_of_offset)
    span = bias.shape[1]
    seq_spec = lambda head_of: pl.BlockSpec((1, seq_len, HEAD_DIM),
                                            lambda b, h: (head_of(h), row_block0 + b, 0))
    in_specs = [pl.BlockSpec(memory_space=pltpu.SMEM),
                seq_spec(lambda h: h),
                seq_spec(lambda h: n_q_heads + h // group),
                seq_spec(lambda h: h // group),
                pl.BlockSpec((tq, span), lambda b, h: (0, 0), pipeline_mode=pl.Buffered(1))]
    args = [sink, qk, qk, v, bias]
    return pl.pallas_call(
        functools.partial(_window_attn_kernel, seq_len=seq_len, tq=tq, chunks=chunks, has_sink=has_sink),
        grid=(n_seq, n_q_heads),
        out_specs=pl.BlockSpec((seq_len, HEAD_DIM), lambda b, h: (row_block0 + b, h)),
        out_shape=jax.ShapeDtypeStruct((qk.shape[1], n_q_heads * HEAD_DIM), BF16),
        scratch_shapes=[pltpu.VMEM((n_sub, tq, span), F32), pltpu.VMEM((n_sub, tq, span), BF16)],
        compiler_params=_params("parallel", "parallel"),
        name=name,
        **_into(out, in_specs, args),
    )(*args)


def _band_bias(tq, chunks, mult_of_offset):
    rows = np.arange(tq)[:, None]
    cols = np.concatenate([rel + np.arange(size) for rel, size in chunks])[None, :]
    mult = mult_of_offset(rows - cols)
    return jnp.asarray(np.where(mult > 0, np.log2(np.maximum(mult, 1).astype(np.float64)), NEG_INF), F32)


def _mult_a(d):
    return (np.abs(d) <= A_HALF_WINDOW).astype(np.int64)


def _mult_c(d):
    mult = np.zeros_like(d)
    for window, dil in C_BRANCHES:
        reach = (window // (2 * dil)) * dil
        mult = mult + ((np.abs(d) <= reach) & (d % dil == 0))
    return mult


def _diff_attn_kernel(q_ref, k_ref, v_ref, lq1_ref, lk1_ref, lq2_ref, lk2_ref, g_ref,
                      *maybe_out_alias_then_out_and_scratch, seq_len, tk, strip, lambda_init):
    o_ref, s_scr, p_scr, acc_scr, m_scr, l_scr, a_scr = maybe_out_alias_then_out_and_scratch[-7:]
    tq = q_ref.shape[1]
    n_lane_blocks = tk // LANE
    rows_per_pass = min(tq, strip)
    m_scr[...] = jnp.full(m_scr.shape, NEG_INF, F32)
    l_scr[...] = jnp.zeros(l_scr.shape, F32)
    acc_scr[...] = jnp.zeros(acc_scr.shape, F32)

    streams = [(c, slice(r0, r0 + rows_per_pass)) for c in range(2) for r0 in range(0, tq, rows_per_pass)]

    def scores(c, rows, k_start):
        s_scr[c, rows, :] = lax.dot_general(q_ref[c, rows, :], k_ref[c, pl.ds(k_start, tk), :], _NT,
                                            preferred_element_type=F32)

    def row_max(c, rows):
        mx = s_scr[c, rows, 0:LANE]
        for cb in range(1, n_lane_blocks):
            mx = jnp.maximum(mx, s_scr[c, rows, cb * LANE:(cb + 1) * LANE])
        m_old = m_scr[c, rows, :]
        m_new = jnp.maximum(m_old, jnp.max(mx, axis=-1, keepdims=True))
        a_scr[c, rows, :] = jnp.exp2(m_old - m_new)
        m_scr[c, rows, :] = m_new

    def probs_and_values(c, rows, k_start):
        m = m_scr[c, rows, :]
        a = a_scr[c, rows, :]
        lsum = a * l_scr[c, rows, :]
        for cb in range(n_lane_blocks):
            cols = slice(cb * LANE, (cb + 1) * LANE)
            p = jnp.exp2(s_scr[c, rows, cols] - m)
            lsum = lsum + p
            p_scr[c, rows, cols] = p.astype(BF16)
        l_scr[c, rows, :] = lsum
        pv = jnp.dot(p_scr[c, rows, :], v_ref[0, pl.ds(k_start, tk), :], preferred_element_type=F32)
        for e in range(v_ref.shape[2] // LANE):
            cols = slice(e * LANE, (e + 1) * LANE)
            acc_scr[c, rows, cols] = a * acc_scr[c, rows, cols] + pv[:, cols]

    def body(kj, carry):
        k_start = pl.multiple_of(kj * tk, tk)
        for c, rows in streams:
            scores(c, rows, k_start)
        for c, rows in streams:
            row_max(c, rows)
        for c, rows in streams:
            probs_and_values(c, rows, k_start)
        return carry

    lax.fori_loop(0, seq_len // tk, body, 0)

    lam = (jnp.exp(jnp.sum(lq1_ref[...] * lk1_ref[...], keepdims=True))
           - jnp.exp(jnp.sum(lq2_ref[...] * lk2_ref[...], keepdims=True)) + lambda_init)
    o1 = acc_scr[0] / jnp.sum(l_scr[0], axis=-1, keepdims=True)
    o2 = acc_scr[1] / jnp.sum(l_scr[1], axis=-1, keepdims=True)
    o = o1 - lam * o2
    o = o * lax.rsqrt(jnp.mean(o * o, axis=-1, keepdims=True) + LN_EPS)
    o_ref[...] = (o * g_ref[...] * (1.0 - lambda_init)).astype(BF16)


def _diff_attn(qk, v, lq1, lk1, lq2, lk2, subln_g, lambda_init, out, *, n_heads, seq_len, n_seq,
               row_block0, tq=1024, tk=2048, strip=512):
    tq = _tile(seq_len, tuple(t for t in (1024, 512, 256, 128) if t <= tq))
    tk = _tile(seq_len, tuple(t for t in (2048, 1024, 512, 256, 128) if t <= tk))
    q_blocks = seq_len // tq
    vd = 2 * HEAD_DIM
    vec = lambda width: pl.BlockSpec((1, width), lambda b, h, i: (0, 0))
    q_row_block = lambda b, i: (row_block0 + b) * q_blocks + i
    in_specs = [pl.BlockSpec((2, tq, HEAD_DIM), lambda b, h, i: (h, q_row_block(b, i), 0)),
                pl.BlockSpec((2, seq_len, HEAD_DIM), lambda b, h, i: (n_heads + h, row_block0 + b, 0),
                             pipeline_mode=pl.Buffered(1)),
                pl.BlockSpec((1, seq_len, vd), lambda b, h, i: (h, row_block0 + b, 0),
                             pipeline_mode=pl.Buffered(1)),
                vec(HEAD_DIM), vec(HEAD_DIM), vec(HEAD_DIM), vec(HEAD_DIM), vec(vd)]
    args = [qk, qk, v, lq1.reshape(1, -1), lk1.reshape(1, -1), lq2.reshape(1, -1), lk2.reshape(1, -1),
            subln_g.reshape(1, -1)]
    return pl.pallas_call(
        functools.partial(_diff_attn_kernel, seq_len=seq_len, tk=tk, strip=strip, lambda_init=lambda_init),
        grid=(n_seq, n_heads, q_blocks),
        out_specs=pl.BlockSpec((tq, vd), lambda b, h, i: (q_row_block(b, i), h)),
        out_shape=jax.ShapeDtypeStruct((qk.shape[1], n_heads * vd), BF16),
        scratch_shapes=[pltpu.VMEM((2, tq, tk), F32), pltpu.VMEM((2, tq, tk), BF16),
                        pltpu.VMEM((2, tq, vd), F32), pltpu.VMEM((2, tq, LANE), F32),
                        pltpu.VMEM((2, tq, LANE), F32), pltpu.VMEM((2, tq, LANE), F32)],
        compiler_params=_params("parallel", "parallel", "arbitrary"),
        name="diff_attn",
        **_into(out, in_specs, args),
    )(*args)


def _rope_tables(max_len, q_scale):
    half = ROT_DIM // 2
    inv_freq = ROPE_THETA ** (-(jnp.arange(0, ROT_DIM, 2, dtype=F32) / ROT_DIM))
    lane = jnp.arange(HEAD_DIM)
    inv_freq_of_lane = jnp.where(lane < ROT_DIM, inv_freq[lane % half], 0.0)
    ang = jnp.arange(max_len, dtype=F32)[:, None] * inv_freq_of_lane[None, :]
    cos, sin = jnp.cos(ang), jnp.sin(ang)
    s1 = jnp.where(lane < half, -sin, 0.0)
    s2 = jnp.where((lane >= half) & (lane < ROT_DIM), sin, 0.0)
    tab = jnp.stack([cos, s1, s2])
    return jnp.stack([tab * q_scale, tab])


def _seq_groups(seq_lens_and_counts):
    groups, row = [], 0
    for seq_len, n_seq in seq_lens_and_counts:
        assert row % seq_len == 0
        groups.append((seq_len, n_seq, row // seq_len))
        row += seq_len * n_seq
    return groups


def kernel(x_prompt, x_sample, a_w_qkv, a_w_o, a_sink, b_w_qkv, b_w_o, b_lambda_q1, b_lambda_k1,
           b_lambda_q2, b_lambda_k2, b_subln_g, c_w_qkv, c_w_o, ln1_g, ln1_b, ln2_g, ln2_b,
           w_gate, w_up, w_down):
    d = x_prompt.shape[-1]
    depth = ln1_g.shape[0]
    n_heads = d // HEAD_DIM
    n_kv_a = n_heads // A_GROUP
    b_heads = n_heads // 2
    alpha = (2.0 * depth) ** 0.25
    q_scale = HEAD_DIM ** -0.5 * LOG2_E

    groups = _seq_groups([(x_prompt.shape[1], x_prompt.shape[0]), (x_sample.shape[1], x_sample.shape[0])])
    x = jnp.concatenate([x_prompt.reshape(-1, d), x_sample.reshape(-1, d)], axis=0)
    xb = x.astype(BF16)
    tab = _rope_tables(max(seq_len for seq_len, _, _ in groups), q_scale)

    reach_c = max((w // (2 * dil)) * dil for w, dil in C_BRANCHES)

    for i in range(depth):
        kind, j = i % N_MIXERS, i // N_MIXERS
        attn = None
        if kind == 0:
            nq = n_heads * HEAD_DIM
            nk = n_kv_a * HEAD_DIM
            w_qkv = a_w_qkv[j].astype(BF16)
            qk = _proj_heads(xb, w_qkv, 0, nq + nk, tab, n_q_cols=nq, groups=groups)
            v = _proj_heads(xb, w_qkv, nq + nk, nk)
            for sl, ns, rb in groups:
                attn = _window_attn(qk, v, a_sink[j], _mult_a, A_HALF_WINDOW, attn, n_q_heads=n_heads,
                                    group=A_GROUP, tq=256, n_sub=4, seq_len=sl, n_seq=ns, row_block0=rb,
                                    name="window_attn_sink")
            w_o = a_w_o[j]
        elif kind == 1:
            lambda_init = 0.8 - 0.6 * math.exp(-0.3 * i)
            w_qkv = b_w_qkv[j].astype(BF16)
            qk = _proj_heads(xb, w_qkv, 0, 2 * d, tab, n_q_cols=d, groups=groups)
            v = _proj_heads(xb, w_qkv, 2 * d, d, chunk=2 * HEAD_DIM)
            for sl, ns, rb in groups:
                attn = _diff_attn(qk, v, b_lambda_q1[j], b_lambda_k1[j], b_lambda_q2[j], b_lambda_k2[j],
                                  b_subln_g[j], lambda_init, attn, n_heads=b_heads, seq_len=sl, n_seq=ns,
                                  row_block0=rb)
            w_o = b_w_o[j]
        else:
            w_qkv = c_w_qkv[j].astype(BF16)
            qk = _proj_heads(xb, w_qkv, 0, 2 * d, tab, n_q_cols=d, groups=groups)
            v = _proj_heads(xb, w_qkv, 2 * d, d)
            for sl, ns, rb in groups:
                attn = _window_attn(qk, v, None, _mult_c, reach_c, attn, n_q_heads=n_heads, group=1,
                                    tq=256, n_sub=4, seq_len=sl, n_seq=ns, row_block0=rb,
                                    name="window_attn_dilated")
            w_o = c_w_o[j]
        h = _matmul(attn, w_o.astype(BF16))
        x, xb = _deepnorm_ln(x, h, ln1_g[i], ln1_b[i], alpha)

        hidden = _gateup(xb, w_gate[i].astype(BF16), w_up[i].astype(BF16))
        h = _matmul(hidden, w_down[i].astype(BF16))
        if i + 1 < depth:
            x, xb = _deepnorm_ln(x, h, ln2_g[i], ln2_b[i], alpha)

    n_prompt = x_prompt.shape[0] * x_prompt.shape[1]
    y_prompt = _deepnorm_ln(x, h, ln2_g[-1], ln2_b[-1], alpha, rows=(0, n_prompt))
    y_sample = _deepnorm_ln(x, h, ln2_g[-1], ln2_b[-1], alpha, rows=(n_prompt, x.shape[0] - n_prompt))
    return y_prompt.reshape(x_prompt.shape), y_sample.reshape(x_sample.shape)
```

```python
import functools
import math

import numpy as np
import jax
import jax.numpy as jnp
from jax import lax
from jax.experimental import pallas as pl
from jax.experimental.pallas import tpu as pltpu

F32 = jnp.float32
BF16 = jnp.bfloat16

LANE = 128
HEAD_DIM = 128
LOG2_E = math.log2(math.e)
ROT_DIM = HEAD_DIM // 4
ROPE_THETA = 500000.0
LN_EPS = 1e-5
NEG_INF = -1e30
N_MIXERS = 3
A_GROUP = 4
A_HALF_WINDOW = 128
C_BRANCHES = ((128, 1), (512, 4), (2048, 16))
VMEM_LIMIT_BYTES = 56 * 1024 * 1024

_NT = (((1,), (1,)), ((), ()))


def _params(*semantics):
    return pltpu.CompilerParams(dimension_semantics=semantics, vmem_limit_bytes=VMEM_LIMIT_BYTES)


def _tile(n, prefs):
    for t in prefs:
        if n % t == 0:
            return t
    raise ValueError(f"no tile in {prefs} divides {n}")


def _proj_rope_kernel(x_ref, w_ref, t_ref, o_ref):
    acc = jnp.dot(x_ref[...], w_ref[...], preferred_element_type=F32)
    cos, s1, s2 = t_ref[0, 0], t_ref[0, 1], t_ref[0, 2]
    half = ROT_DIM // 2
    for c in range(o_ref.shape[0]):
        blk = acc[:, c * HEAD_DIM:(c + 1) * HEAD_DIM]
        blk = (blk * cos + pltpu.roll(blk, HEAD_DIM - half, 1) * s1 + pltpu.roll(blk, half, 1) * s2)
        o_ref[c] = blk.astype(BF16)


def _proj_plain_kernel(x_ref, w_ref, o_ref):
    acc = jnp.dot(x_ref[...], w_ref[...], preferred_element_type=F32)
    chunk = o_ref.shape[2]
    for c in range(o_ref.shape[0]):
        o_ref[c] = acc[:, c * chunk:(c + 1) * chunk].astype(BF16)


def _proj_heads(xb, w, layer, col0, n, tab=None, n_q_cols=0, groups=(), chunk=HEAD_DIM):
    m, k = xb.shape
    tm = _tile(math.gcd(m, *(sl for sl, _, _ in groups)), (1024, 512, 256))
    tn = _tile(math.gcd(n, n_q_cols, col0), (1024, 512, 256, 128))
    grid = (m // tm, n // tn)
    col_tile0 = col0 // tn
    in_specs = [pl.BlockSpec((tm, k), lambda i, j: (i, 0)),
                pl.BlockSpec((None, k, tn), lambda i, j: (layer, 0, col_tile0 + j))]
    args = [xb, w]
    if tab is not None:
        assert chunk == HEAD_DIM
        n_q_tiles = n_q_cols // tn

        def position_tile(i):
            pos, tile0 = i, 0
            for seq_len, n_seq, _ in groups:
                pos = jnp.where(i >= tile0, (i - tile0) % (seq_len // tm), pos)
                tile0 += seq_len * n_seq // tm
            return pos

        in_specs.append(pl.BlockSpec((1, 3, tm, HEAD_DIM),
                                     lambda i, j: (jnp.where(j < n_q_tiles, 0, 1), 0, position_tile(i), 0)))
        args.append(tab)
        body = _proj_rope_kernel
    else:
        body = _proj_plain_kernel
    return pl.pallas_call(
        body,
        grid=grid,
        in_specs=in_specs,
        out_specs=pl.BlockSpec((tn // chunk, tm, chunk), lambda i, j: (j, i, 0)),
        out_shape=jax.ShapeDtypeStruct((n // chunk, m, chunk), BF16),
        compiler_params=_params("parallel", "parallel"),
        name="proj_heads_rope" if tab is not None else "proj_heads",
    )(*args)


def _mm_kernel(a_ref, w_ref, o_ref):
    o_ref[...] = jnp.dot(a_ref[...], w_ref[...], preferred_element_type=F32).astype(o_ref.dtype)


def _matmul(a, w, layer, out_dtype=F32):
    m, k = a.shape
    n = w.shape[2]
    if 8 * 1024 * k > VMEM_LIMIT_BYTES * 3 // 4:
        tm = _tile(m, (512, 256))
        tn = _tile(n, (1024, 512, 256, 128))
        return pl.pallas_call(
            _mm_kernel,
            grid=(n // tn, m // tm),
            in_specs=[pl.BlockSpec((tm, k), lambda j, i: (i, 0)),
                      pl.BlockSpec((None, k, tn), lambda j, i: (layer, 0, j), pipeline_mode=pl.Buffered(1))],
            out_specs=pl.BlockSpec((tm, tn), lambda j, i: (i, j)),
            out_shape=jax.ShapeDtypeStruct((m, n), out_dtype),
            compiler_params=_params("parallel", "parallel"),
            name="matmul_long_k",
        )(a, w)
    tm = _tile(m, (1024, 512, 256))
    tn = _tile(n, (1024, 512, 256, 128))
    return pl.pallas_call(
        _mm_kernel,
        grid=(m // tm, n // tn),
        in_specs=[pl.BlockSpec((tm, k), lambda i, j: (i, 0)),
                  pl.BlockSpec((None, k, tn), lambda i, j: (layer, 0, j))],
        out_specs=pl.BlockSpec((tm, tn), lambda i, j: (i, j)),
        out_shape=jax.ShapeDtypeStruct((m, n), out_dtype),
        compiler_params=_params("parallel", "parallel"),
        name="matmul",
    )(a, w)


def _gateup_kernel(x_ref, wg_ref, wu_ref, o_ref):
    x = x_ref[...]
    g = jnp.dot(x, wg_ref[...], preferred_element_type=F32)
    u = jnp.dot(x, wu_ref[...], preferred_element_type=F32)
    o_ref[...] = (g * jax.nn.sigmoid(g) * u).astype(BF16)


def _gateup(xb, wg, wu, layer):
    m, k = xb.shape
    f = wg.shape[2]
    tn = _tile(f, (512, 256, 128))
    tm = _tile(m, (2048, 1024, 512, 256) if tn <= 256 else (1024, 512, 256))
    return pl.pallas_call(
        _gateup_kernel,
        grid=(m // tm, f // tn),
        in_specs=[pl.BlockSpec((tm, k), lambda i, j: (i, 0)),
                  pl.BlockSpec((None, k, tn), lambda i, j: (layer, 0, j)),
                  pl.BlockSpec((None, k, tn), lambda i, j: (layer, 0, j))],
        out_specs=pl.BlockSpec((tm, tn), lambda i, j: (i, j)),
        out_shape=jax.ShapeDtypeStruct((m, f), BF16),
        compiler_params=_params("parallel", "parallel"),
        name="ffn_gate_up",
    )(xb, wg, wu)


def _ln_kernel(x_ref, h_ref, g_ref, b_ref, xo_ref, *maybe_xb_ref, alpha):
    y = alpha * x_ref[...] + h_ref[...]
    mu = jnp.mean(y, axis=-1, keepdims=True)
    yc = y - mu
    var = jnp.mean(yc * yc, axis=-1, keepdims=True)
    out = yc * lax.rsqrt(var + LN_EPS) * g_ref[...] + b_ref[...]
    xo_ref[...] = out
    for xb_ref in maybe_xb_ref:
        xb_ref[...] = out.astype(BF16)


def _deepnorm_ln(x, h, g, b, alpha, rows=None):
    m, d = x.shape
    row0, n_rows = rows or (0, m)
    tm = _tile(math.gcd(row0, n_rows), (256, 128))
    tile0 = row0 // tm
    row_in = pl.BlockSpec((tm, d), lambda i: (tile0 + i, 0))
    row_out = pl.BlockSpec((tm, d), lambda i: (i, 0))
    vec = pl.BlockSpec((1, d), lambda i: (0, 0))
    out_dtypes = (F32,) if rows else (F32, BF16)
    outs = pl.pallas_call(
        functools.partial(_ln_kernel, alpha=alpha),
        grid=(n_rows // tm,),
        in_specs=[row_in, row_in, vec, vec],
        out_specs=[row_out] * len(out_dtypes),
        out_shape=[jax.ShapeDtypeStruct((n_rows, d), dt) for dt in out_dtypes],
        compiler_params=_params("parallel"),
        name="deepnorm_ln",
    )(x, h, g.reshape(1, d), b.reshape(1, d))
    return outs[0] if rows else outs


def _into(out, in_specs, args):
    if out is None:
        return dict(in_specs=in_specs)
    args.append(out)
    return dict(in_specs=in_specs + [pl.BlockSpec(memory_space=pl.ANY)],
                input_output_aliases={len(args) - 1: 0})


def _window_attn_kernel(sink_ref, q_ref, k_ref, v_ref, bias_ref, *maybe_out_alias_then_out_and_scratch,
                        seq_len, tq, chunks, has_sink):
    o_ref, s_scr, p_scr = maybe_out_alias_then_out_and_scratch[-3:]
    head = pl.program_id(1)
    n_sub = s_scr.shape[0]
    n_lane_blocks = s_scr.shape[2] // LANE

    def scores(sub, q_start):
        q = q_ref[0, pl.ds(q_start, tq), :]
        starts = []
        col = 0
        for rel, size in chunks:
            start = q_start + rel
            inside = jnp.logical_and(start >= 0, start + size <= seq_len)
            cstart = pl.multiple_of(jnp.clip(start, 0, seq_len - size), HEAD_DIM)
            starts.append(cstart)
            kj = k_ref[0, pl.ds(cstart, size), :]
            sj = lax.dot_general(q, kj, _NT, preferred_element_type=F32)
            pen = jnp.where(inside, 0.0, NEG_INF).astype(F32)
            s_scr[sub, :, col:col + size] = sj + bias_ref[:, col:col + size] + pen
            col += size
        return starts

    def softmax(sub):
        m = s_scr[sub, :, 0:LANE]
        for cb in range(1, n_lane_blocks):
            m = jnp.maximum(m, s_scr[sub, :, cb * LANE:(cb + 1) * LANE])
        m = jnp.broadcast_to(jnp.max(m, axis=-1, keepdims=True), (tq, LANE))
        if has_sink:
            sk = sink_ref[head] * LOG2_E
            m = jnp.maximum(m, sk)
        lsum = jnp.zeros((tq, LANE), F32)
        for cb in range(n_lane_blocks):
            cols = slice(cb * LANE, (cb + 1) * LANE)
            p = jnp.exp2(s_scr[sub, :, cols] - m)
            lsum = lsum + p
            p_scr[sub, :, cols] = p.astype(BF16)
        denom = jnp.sum(lsum, axis=-1, keepdims=True)
        if has_sink:
            denom = denom + jnp.exp2(sk - m[:, 0:1])
        return denom

    def weighted_values(sub, q_start, starts, denom):
        acc = jnp.zeros((tq, HEAD_DIM), F32)
        col = 0
        for (rel, size), cstart in zip(chunks, starts):
            vj = v_ref[0, pl.ds(cstart, size), :]
            acc = acc + jnp.dot(p_scr[sub, :, col:col + size], vj, preferred_element_type=F32)
            col += size
        o_ref[pl.ds(q_start, tq), :] = (acc / denom).astype(BF16)

    def body(qi, carry):
        q_starts = [pl.multiple_of((qi * n_sub + sub) * tq, tq) for sub in range(n_sub)]
        starts = [scores(sub, q_starts[sub]) for sub in range(n_sub)]
        denoms = [softmax(sub) for sub in range(n_sub)]
        for sub in range(n_sub):
            weighted_values(sub, q_starts[sub], starts[sub], denoms[sub])
        return carry

    lax.fori_loop(0, seq_len // (n_sub * tq), body, 0)


def _window_attn(qk, v, sink, mult_of_offset, reach, out, *, n_q_heads, group, tq, n_sub, seq_len, n_seq,
                 row_block0, name):
    has_sink = sink is not None
    if not has_sink:
        sink = jnp.zeros((n_q_heads,), F32)
    piece = min(reach, tq)
    assert reach % piece == 0 and tq % piece == 0 and seq_len % (n_sub * tq) == 0
    side = reach // piece
    chunks = (tuple((-reach + i * piece, piece) for i in range(side)) + ((0, tq),)
              + tuple((tq + i * piece, piece) for i in range(side)))
    bias = _band_bias(tq, chunks, mult_of_offset)
    span = bias.shape[1]
    seq_spec = lambda head_of: pl.BlockSpec((1, seq_len, HEAD_DIM),
                                            lambda b, h: (head_of(h), row_block0 + b, 0))
    in_specs = [pl.BlockSpec(memory_space=pltpu.SMEM),
                seq_spec(lambda h: h),
                seq_spec(lambda h: n_q_heads + h // group),
                seq_spec(lambda h: h // group),
                pl.BlockSpec((tq, span), lambda b, h: (0, 0), pipeline_mode=pl.Buffered(1))]
    args = [sink, qk, qk, v, bias]
    return pl.pallas_call(
        functools.partial(_window_attn_kernel, seq_len=seq_len, tq=tq, chunks=chunks, has_sink=has_sink),
        grid=(n_seq, n_q_heads),
        out_specs=pl.BlockSpec((seq_len, HEAD_DIM), lambda b, h: (row_block0 + b, h)),
        out_shape=jax.ShapeDtypeStruct((qk.shape[1], n_q_heads * HEAD_DIM), BF16),
        scratch_shapes=[pltpu.VMEM((n_sub, tq, span), F32), pltpu.VMEM((n_sub, tq, span), BF16)],
        compiler_params=_params("parallel", "parallel"),
        name=name,
        **_into(out, in_specs, args),
    )(*args)


def _band_bias(tq, chunks, mult_of_offset):
    rows = np.arange(tq)[:, None]
    cols = np.concatenate([rel + np.arange(size) for rel, size in chunks])[None, :]
    mult = mult_of_offset(rows - cols)
    return jnp.asarray(np.where(mult > 0, np.log2(np.maximum(mult, 1).astype(np.float64)), NEG_INF), F32)


def _mult_a(d):
    return (np.abs(d) <= A_HALF_WINDOW).astype(np.int64)


def _mult_c(d):
    mult = np.zeros_like(d)
    for window, dil in C_BRANCHES:
        reach = (window // (2 * dil)) * dil
        mult = mult + ((np.abs(d) <= reach) & (d % dil == 0))
    return mult


def _diff_attn_kernel(q_ref, k_ref, v_ref, lq1_ref, lk1_ref, lq2_ref, lk2_ref, g_ref,
                      *maybe_out_alias_then_out_and_scratch, seq_len, tk, strip, lambda_init):
    o_ref, s_scr, p_scr, acc_scr, m_scr, l_scr, a_scr = maybe_out_alias_then_out_and_scratch[-7:]
    tq = q_ref.shape[1]
    n_lane_blocks = tk // LANE
    rows_per_pass = min(tq, strip)
    m_scr[...] = jnp.full(m_scr.shape, NEG_INF, F32)
    l_scr[...] = jnp.zeros(l_scr.shape, F32)
    acc_scr[...] = jnp.zeros(acc_scr.shape, F32)

    streams = [(c, slice(r0, r0 + rows_per_pass)) for c in range(2) for r0 in range(0, tq, rows_per_pass)]

    def scores(c, rows, k_start):
        s_scr[c, rows, :] = lax.dot_general(q_ref[c, rows, :], k_ref[c, pl.ds(k_start, tk), :], _NT,
                                            preferred_element_type=F32)

    def row_max(c, rows):
        mx = s_scr[c, rows, 0:LANE]
        for cb in range(1, n_lane_blocks):
            mx = jnp.maximum(mx, s_scr[c, rows, cb * LANE:(cb + 1) * LANE])
        m_old = m_scr[c, rows, :]
        m_new = jnp.maximum(m_old, jnp.max(mx, axis=-1, keepdims=True))
        a_scr[c, rows, :] = jnp.exp2(m_old - m_new)
        m_scr[c, rows, :] = m_new

    def probs_and_values(c, rows, k_start):
        m = m_scr[c, rows, :]
        a = a_scr[c, rows, :]
        lsum = a * l_scr[c, rows, :]
        for cb in range(n_lane_blocks):
            cols = slice(cb * LANE, (cb + 1) * LANE)
            p = jnp.exp2(s_scr[c, rows, cols] - m)
            lsum = lsum + p
            p_scr[c, rows, cols] = p.astype(BF16)
        l_scr[c, rows, :] = lsum
        pv = jnp.dot(p_scr[c, rows, :], v_ref[0, pl.ds(k_start, tk), :], preferred_element_type=F32)
        for e in range(v_ref.shape[2] // LANE):
            cols = slice(e * LANE, (e + 1) * LANE)
            acc_scr[c, rows, cols] = a * acc_scr[c, rows, cols] + pv[:, cols]

    def body(kj, carry):
        k_start = pl.multiple_of(kj * tk, tk)
        for c, rows in streams:
            scores(c, rows, k_start)
        for c, rows in streams:
            row_max(c, rows)
        for c, rows in streams:
            probs_and_values(c, rows, k_start)
        return carry

    lax.fori_loop(0, seq_len // tk, body, 0)

    lam = (jnp.exp(jnp.sum(lq1_ref[...] * lk1_ref[...], keepdims=True))
           - jnp.exp(jnp.sum(lq2_ref[...] * lk2_ref[...], keepdims=True)) + lambda_init)
    o1 = acc_scr[0] / jnp.sum(l_scr[0], axis=-1, keepdims=True)
    o2 = acc_scr[1] / jnp.sum(l_scr[1], axis=-1, keepdims=True)
    o = o1 - lam * o2
    o = o * lax.rsqrt(jnp.mean(o * o, axis=-1, keepdims=True) + LN_EPS)
    o_ref[...] = (o * g_ref[...] * (1.0 - lambda_init)).astype(BF16)


def _diff_attn(qk, v, lq1, lk1, lq2, lk2, subln_g, lambda_init, out, *, n_heads, seq_len, n_seq,
               row_block0, tq=1024, tk=2048, strip=512):
    tq = _tile(seq_len, tuple(t for t in (1024, 512, 256, 128) if t <= tq))
    tk = _tile(seq_len, tuple(t for t in (2048, 1024, 512, 256, 128) if t <= tk))
    q_blocks = seq_len // tq
    vd = 2 * HEAD_DIM
    vec = lambda width: pl.BlockSpec((1, width), lambda b, h, i: (0, 0))
    q_row_block = lambda b, i: (row_block0 + b) * q_blocks + i
    in_specs = [pl.BlockSpec((2, tq, HEAD_DIM), lambda b, h, i: (h, q_row_block(b, i), 0)),
                pl.BlockSpec((2, seq_len, HEAD_DIM), lambda b, h, i: (n_heads + h, row_block0 + b, 0),
                             pipeline_mode=pl.Buffered(1)),
                pl.BlockSpec((1, seq_len, vd), lambda b, h, i: (h, row_block0 + b, 0),
                             pipeline_mode=pl.Buffered(1)),
                vec(HEAD_DIM), vec(HEAD_DIM), vec(HEAD_DIM), vec(HEAD_DIM), vec(vd)]
    args = [qk, qk, v, lq1.reshape(1, -1), lk1.reshape(1, -1), lq2.reshape(1, -1), lk2.reshape(1, -1),
            subln_g.reshape(1, -1)]
    return pl.pallas_call(
        functools.partial(_diff_attn_kernel, seq_len=seq_len, tk=tk, strip=strip, lambda_init=lambda_init),
        grid=(n_seq, n_heads, q_blocks),
        out_specs=pl.BlockSpec((tq, vd), lambda b, h, i: (q_row_block(b, i), h)),
        out_shape=jax.ShapeDtypeStruct((qk.shape[1], n_heads * vd), BF16),
        scratch_shapes=[pltpu.VMEM((2, tq, tk), F32), pltpu.VMEM((2, tq, tk), BF16),
                        pltpu.VMEM((2, tq, vd), F32), pltpu.VMEM((2, tq, LANE), F32),
                        pltpu.VMEM((2, tq, LANE), F32), pltpu.VMEM((2, tq, LANE), F32)],
        compiler_params=_params("parallel", "parallel", "arbitrary"),
        name="diff_attn",
        **_into(out, in_specs, args),
    )(*args)


def _rope_tables(max_len, q_scale):
    half = ROT_DIM // 2
    inv_freq = ROPE_THETA ** (-(jnp.arange(0, ROT_DIM, 2, dtype=F32) / ROT_DIM))
    lane = jnp.arange(HEAD_DIM)
    inv_freq_of_lane = jnp.where(lane < ROT_DIM, inv_freq[lane % half], 0.0)
    ang = jnp.arange(max_len, dtype=F32)[:, None] * inv_freq_of_lane[None, :]
    cos, sin = jnp.cos(ang), jnp.sin(ang)
    s1 = jnp.where(lane < half, -sin, 0.0)
    s2 = jnp.where((lane >= half) & (lane < ROT_DIM), sin, 0.0)
    tab = jnp.stack([cos, s1, s2])
    return jnp.stack([tab * q_scale, tab])


def _seq_groups(seq_lens_and_counts):
    groups, row = [], 0
    for seq_len, n_seq in seq_lens_and_counts:
        assert row % seq_len == 0
        groups.append((seq_len, n_seq, row // seq_len))
        row += seq_len * n_seq
    return groups


def kernel(x_prompt, x_sample, a_w_qkv, a_w_o, a_sink, b_w_qkv, b_w_o, b_lambda_q1, b_lambda_k1,
           b_lambda_q2, b_lambda_k2, b_subln_g, c_w_qkv, c_w_o, ln1_g, ln1_b, ln2_g, ln2_b,
           w_gate, w_up, w_down):
    d = x_prompt.shape[-1]
    depth = ln1_g.shape[0]
    n_heads = d // HEAD_DIM
    n_kv_a = n_heads // A_GROUP
    b_heads = n_heads // 2
    alpha = (2.0 * depth) ** 0.25
    q_scale = HEAD_DIM ** -0.5 * LOG2_E

    groups = _seq_groups([(x_prompt.shape[1], x_prompt.shape[0]), (x_sample.shape[1], x_sample.shape[0])])
    x = jnp.concatenate([x_prompt.reshape(-1, d), x_sample.reshape(-1, d)], axis=0)
    xb = x.astype(BF16)
    tab = _rope_tables(max(seq_len for seq_len, _, _ in groups), q_scale)

    reach_c = max((w // (2 * dil)) * dil for w, dil in C_BRANCHES)
    a_w_qkv, a_w_o, b_w_qkv, b_w_o, c_w_qkv, c_w_o, w_gate, w_up, w_down = (
        w.astype(BF16) for w in (a_w_qkv, a_w_o, b_w_qkv, b_w_o, c_w_qkv, c_w_o, w_gate, w_up, w_down))

    for i in range(depth):
        kind, j = i % N_MIXERS, i // N_MIXERS
        attn = None
        if kind == 0:
            nq = n_heads * HEAD_DIM
            nk = n_kv_a * HEAD_DIM
            qk = _proj_heads(xb, a_w_qkv, j, 0, nq + nk, tab, n_q_cols=nq, groups=groups)
            v = _proj_heads(xb, a_w_qkv, j, nq + nk, nk)
            for sl, ns, rb in groups:
                attn = _window_attn(qk, v, a_sink[j], _mult_a, A_HALF_WINDOW, attn, n_q_heads=n_heads,
                                    group=A_GROUP, tq=256, n_sub=4, seq_len=sl, n_seq=ns, row_block0=rb,
                                    name="window_attn_sink")
            w_o = a_w_o
        elif kind == 1:
            lambda_init = 0.8 - 0.6 * math.exp(-0.3 * i)
            qk = _proj_heads(xb, b_w_qkv, j, 0, 2 * d, tab, n_q_cols=d, groups=groups)
            v = _proj_heads(xb, b_w_qkv, j, 2 * d, d, chunk=2 * HEAD_DIM)
            for sl, ns, rb in groups:
                attn = _diff_attn(qk, v, b_lambda_q1[j], b_lambda_k1[j], b_lambda_q2[j], b_lambda_k2[j],
                                  b_subln_g[j], lambda_init, attn, n_heads=b_heads, seq_len=sl, n_seq=ns,
                                  row_block0=rb)
            w_o = b_w_o
        else:
            qk = _proj_heads(xb, c_w_qkv, j, 0, 2 * d, tab, n_q_cols=d, groups=groups)
            v = _proj_heads(xb, c_w_qkv, j, 2 * d, d)
            for sl, ns, rb in groups:
                attn = _window_attn(qk, v, None, _mult_c, reach_c, attn, n_q_heads=n_heads, group=1,
                                    tq=256, n_sub=4, seq_len=sl, n_seq=ns, row_block0=rb,
                                    name="window_attn_dilated")
            w_o = c_w_o
        h = _matmul(attn, w_o, j)
        x, xb = _deepnorm_ln(x, h, ln1_g[i], ln1_b[i], alpha)

        hidden = _gateup(xb, w_gate, w_up, i)
        h = _matmul(hidden, w_down, i)
        if i + 1 < depth:
            x, xb = _deepnorm_ln(x, h, ln2_g[i], ln2_b[i], alpha)

    n_prompt = x_prompt.shape[0] * x_prompt.shape[1]
    y_prompt = _deepnorm_ln(x, h, ln2_g[-1], ln2_b[-1], alpha, rows=(0, n_prompt))
    y_sample = _deepnorm_ln(x, h, ln2_g[-1], ln2_b[-1], alpha, rows=(n_prompt, x.shape[0] - n_prompt))
    return y_prompt.reshape(x_prompt.shape), y_sample.reshape(x_sample.shape)
```

```python
import functools
import math

import numpy as np
import jax
import jax.numpy as jnp
from jax import lax
from jax.experimental import pallas as pl
from jax.experimental.pallas import tpu as pltpu

F32 = jnp.float32
BF16 = jnp.bfloat16

LANE = 128
HEAD_DIM = 128
LOG2_E = math.log2(math.e)
ROT_DIM = HEAD_DIM // 4
ROPE_THETA = 500000.0
LN_EPS = 1e-5
NEG_INF = -1e30
MAX_FIXED_MAX_SLACK = 80.0
N_MIXERS = 3
A_GROUP = 4
A_HALF_WINDOW = 128
C_BRANCHES = ((128, 1), (512, 4), (2048, 16))
VMEM_LIMIT_BYTES = 56 * 1024 * 1024

_NT = (((1,), (1,)), ((), ()))


def _params(*semantics):
    return pltpu.CompilerParams(dimension_semantics=semantics, vmem_limit_bytes=VMEM_LIMIT_BYTES)


def _tile(n, prefs):
    for t in prefs:
        if n % t == 0:
            return t
    raise ValueError(f"no tile in {prefs} divides {n}")


def _proj_rope_kernel(x_ref, w_ref, t_ref, o_ref):
    acc = jnp.dot(x_ref[...], w_ref[...], preferred_element_type=F32)
    cos, s1, s2 = t_ref[0, 0], t_ref[0, 1], t_ref[0, 2]
    half = ROT_DIM // 2
    for c in range(o_ref.shape[0]):
        blk = acc[:, c * HEAD_DIM:(c + 1) * HEAD_DIM]
        blk = (blk * cos + pltpu.roll(blk, HEAD_DIM - half, 1) * s1 + pltpu.roll(blk, half, 1) * s2)
        o_ref[c] = blk.astype(BF16)


def _proj_plain_kernel(x_ref, w_ref, o_ref):
    acc = jnp.dot(x_ref[...], w_ref[...], preferred_element_type=F32)
    chunk = o_ref.shape[2]
    for c in range(o_ref.shape[0]):
        o_ref[c] = acc[:, c * chunk:(c + 1) * chunk].astype(BF16)


def _proj_heads(xb, w, layer, col0, n, tab=None, n_q_cols=0, groups=(), chunk=HEAD_DIM):
    m, k = xb.shape
    tm = _tile(math.gcd(m, *(sl for sl, _, _ in groups)), (1024, 512, 256))
    tn = _tile(math.gcd(n, n_q_cols, col0), (1024, 512, 256, 128))
    grid = (m // tm, n // tn)
    col_tile0 = col0 // tn
    in_specs = [pl.BlockSpec((tm, k), lambda i, j: (i, 0)),
                pl.BlockSpec((None, k, tn), lambda i, j: (layer, 0, col_tile0 + j))]
    args = [xb, w]
    if tab is not None:
        assert chunk == HEAD_DIM
        n_q_tiles = n_q_cols // tn

        def position_tile(i):
            pos, tile0 = i, 0
            for seq_len, n_seq, _ in groups:
                pos = jnp.where(i >= tile0, (i - tile0) % (seq_len // tm), pos)
                tile0 += seq_len * n_seq // tm
            return pos

        in_specs.append(pl.BlockSpec((1, 3, tm, HEAD_DIM),
                                     lambda i, j: (jnp.where(j < n_q_tiles, 0, 1), 0, position_tile(i), 0)))
        args.append(tab)
        body = _proj_rope_kernel
    else:
        body = _proj_plain_kernel
    return pl.pallas_call(
        body,
        grid=grid,
        in_specs=in_specs,
        out_specs=pl.BlockSpec((tn // chunk, tm, chunk), lambda i, j: (j, i, 0)),
        out_shape=jax.ShapeDtypeStruct((n // chunk, m, chunk), BF16),
        compiler_params=_params("parallel", "parallel"),
        name="proj_heads_rope" if tab is not None else "proj_heads",
    )(*args)


def _mm_kernel(a_ref, w_ref, o_ref):
    o_ref[...] = jnp.dot(a_ref[...], w_ref[...], preferred_element_type=F32).astype(o_ref.dtype)


def _matmul(a, w, layer, out_dtype=F32):
    m, k = a.shape
    n = w.shape[2]
    if 8 * 1024 * k > VMEM_LIMIT_BYTES * 3 // 4:
        tm = _tile(m, (512, 256))
        tn = _tile(n, (1024, 512, 256, 128))
        return pl.pallas_call(
            _mm_kernel,
            grid=(n // tn, m // tm),
            in_specs=[pl.BlockSpec((tm, k), lambda j, i: (i, 0)),
                      pl.BlockSpec((None, k, tn), lambda j, i: (layer, 0, j), pipeline_mode=pl.Buffered(1))],
            out_specs=pl.BlockSpec((tm, tn), lambda j, i: (i, j)),
            out_shape=jax.ShapeDtypeStruct((m, n), out_dtype),
            compiler_params=_params("parallel", "parallel"),
            name="matmul_long_k",
        )(a, w)
    tm = _tile(m, (1024, 512, 256))
    tn = _tile(n, (1024, 512, 256, 128))
    return pl.pallas_call(
        _mm_kernel,
        grid=(m // tm, n // tn),
        in_specs=[pl.BlockSpec((tm, k), lambda i, j: (i, 0)),
                  pl.BlockSpec((None, k, tn), lambda i, j: (layer, 0, j))],
        out_specs=pl.BlockSpec((tm, tn), lambda i, j: (i, j)),
        out_shape=jax.ShapeDtypeStruct((m, n), out_dtype),
        compiler_params=_params("parallel", "parallel"),
        name="matmul",
    )(a, w)


def _gateup_kernel(x_ref, wg_ref, wu_ref, o_ref):
    x = x_ref[...]
    g = jnp.dot(x, wg_ref[...], preferred_element_type=F32)
    u = jnp.dot(x, wu_ref[...], preferred_element_type=F32)
    o_ref[...] = (g * jax.nn.sigmoid(g) * u).astype(BF16)


def _gateup(xb, wg, wu, layer):
    m, k = xb.shape
    f = wg.shape[2]
    tn = _tile(f, (512, 256, 128))
    tm = _tile(m, (2048, 1024, 512, 256) if tn <= 256 else (1024, 512, 256))
    return pl.pallas_call(
        _gateup_kernel,
        grid=(m // tm, f // tn),
        in_specs=[pl.BlockSpec((tm, k), lambda i, j: (i, 0)),
                  pl.BlockSpec((None, k, tn), lambda i, j: (layer, 0, j)),
                  pl.BlockSpec((None, k, tn), lambda i, j: (layer, 0, j))],
        out_specs=pl.BlockSpec((tm, tn), lambda i, j: (i, j)),
        out_shape=jax.ShapeDtypeStruct((m, f), BF16),
        compiler_params=_params("parallel", "parallel"),
        name="ffn_gate_up",
    )(xb, wg, wu)


def _ln_kernel(x_ref, h_ref, g_ref, b_ref, xo_ref, *maybe_xb_ref, alpha):
    y = alpha * x_ref[...] + h_ref[...]
    mu = jnp.mean(y, axis=-1, keepdims=True)
    yc = y - mu
    var = jnp.mean(yc * yc, axis=-1, keepdims=True)
    out = yc * lax.rsqrt(var + LN_EPS) * g_ref[...] + b_ref[...]
    xo_ref[...] = out
    for xb_ref in maybe_xb_ref:
        xb_ref[...] = out.astype(BF16)


def _deepnorm_ln(x, h, g, b, alpha, rows=None):
    m, d = x.shape
    row0, n_rows = rows or (0, m)
    tm = _tile(math.gcd(row0, n_rows), (256, 128))
    tile0 = row0 // tm
    row_in = pl.BlockSpec((tm, d), lambda i: (tile0 + i, 0))
    row_out = pl.BlockSpec((tm, d), lambda i: (i, 0))
    vec = pl.BlockSpec((1, d), lambda i: (0, 0))
    out_dtypes = (F32,) if rows else (F32, BF16)
    outs = pl.pallas_call(
        functools.partial(_ln_kernel, alpha=alpha),
        grid=(n_rows // tm,),
        in_specs=[row_in, row_in, vec, vec],
        out_specs=[row_out] * len(out_dtypes),
        out_shape=[jax.ShapeDtypeStruct((n_rows, d), dt) for dt in out_dtypes],
        compiler_params=_params("parallel"),
        name="deepnorm_ln",
    )(x, h, g.reshape(1, d), b.reshape(1, d))
    return outs[0] if rows else outs


def _into(out, in_specs, args):
    if out is None:
        return dict(in_specs=in_specs)
    args.append(out)
    return dict(in_specs=in_specs + [pl.BlockSpec(memory_space=pl.ANY)],
                input_output_aliases={len(args) - 1: 0})


def _window_attn_kernel(sink_ref, q_ref, k_ref, v_ref, bias_ref, *maybe_out_alias_then_out_and_scratch,
                        seq_len, tq, chunks, has_sink):
    o_ref, s_scr, p_scr = maybe_out_alias_then_out_and_scratch[-3:]
    head = pl.program_id(1)
    n_sub = s_scr.shape[0]
    n_lane_blocks = s_scr.shape[2] // LANE

    def scores(sub, q_start):
        q = q_ref[0, pl.ds(q_start, tq), :]
        starts = []
        col = 0
        for rel, size in chunks:
            start = q_start + rel
            inside = jnp.logical_and(start >= 0, start + size <= seq_len)
            cstart = pl.multiple_of(jnp.clip(start, 0, seq_len - size), HEAD_DIM)
            starts.append(cstart)
            kj = k_ref[0, pl.ds(cstart, size), :]
            sj = lax.dot_general(q, kj, _NT, preferred_element_type=F32)
            pen = jnp.where(inside, 0.0, NEG_INF).astype(F32)
            s_scr[sub, :, col:col + size] = sj + bias_ref[:, col:col + size] + pen
            col += size
        return starts

    def softmax(sub):
        m = s_scr[sub, :, 0:LANE]
        for cb in range(1, n_lane_blocks):
            m = jnp.maximum(m, s_scr[sub, :, cb * LANE:(cb + 1) * LANE])
        m = jnp.broadcast_to(jnp.max(m, axis=-1, keepdims=True), (tq, LANE))
        if has_sink:
            sk = sink_ref[head] * LOG2_E
            m = jnp.maximum(m, sk)
        lsum = jnp.zeros((tq, LANE), F32)
        for cb in range(n_lane_blocks):
            cols = slice(cb * LANE, (cb + 1) * LANE)
            p = jnp.exp2(s_scr[sub, :, cols] - m)
            lsum = lsum + p
            p_scr[sub, :, cols] = p.astype(BF16)
        denom = jnp.sum(lsum, axis=-1, keepdims=True)
        if has_sink:
            denom = denom + jnp.exp2(sk - m[:, 0:1])
        return denom

    def weighted_values(sub, q_start, starts, denom):
        acc = jnp.zeros((tq, HEAD_DIM), F32)
        col = 0
        for (rel, size), cstart in zip(chunks, starts):
            vj = v_ref[0, pl.ds(cstart, size), :]
            acc = acc + jnp.dot(p_scr[sub, :, col:col + size], vj, preferred_element_type=F32)
            col += size
        o_ref[pl.ds(q_start, tq), :] = (acc / denom).astype(BF16)

    def body(qi, carry):
        q_starts = [pl.multiple_of((qi * n_sub + sub) * tq, tq) for sub in range(n_sub)]
        starts = [scores(sub, q_starts[sub]) for sub in range(n_sub)]
        denoms = [softmax(sub) for sub in range(n_sub)]
        for sub in range(n_sub):
            weighted_values(sub, q_starts[sub], starts[sub], denoms[sub])
        return carry

    lax.fori_loop(0, seq_len // (n_sub * tq), body, 0)


def _window_attn(qk, v, sink, mult_of_offset, reach, out, *, n_q_heads, group, tq, n_sub, seq_len, n_seq,
                 row_block0, name):
    has_sink = sink is not None
    if not has_sink:
        sink = jnp.zeros((n_q_heads,), F32)
    piece = min(reach, tq)
    assert reach % piece == 0 and tq % piece == 0 and seq_len % (n_sub * tq) == 0
    side = reach // piece
    chunks = (tuple((-reach + i * piece, piece) for i in range(side)) + ((0, tq),)
              + tuple((tq + i * piece, piece) for i in range(side)))
    bias = _band_bias(tq, chunks, mult_of_offset)
    span = bias.shape[1]
    seq_spec = lambda head_of: pl.BlockSpec((1, seq_len, HEAD_DIM),
                                            lambda b, h: (head_of(h), row_block0 + b, 0))
    in_specs = [pl.BlockSpec(memory_space=pltpu.SMEM),
                seq_spec(lambda h: h),
                seq_spec(lambda h: n_q_heads + h // group),
                seq_spec(lambda h: h // group),
                pl.BlockSpec((tq, span), lambda b, h: (0, 0), pipeline_mode=pl.Buffered(1))]
    args = [sink, qk, qk, v, bias]
    return pl.pallas_call(
        functools.partial(_window_attn_kernel, seq_len=seq_len, tq=tq, chunks=chunks, has_sink=has_sink),
        grid=(n_seq, n_q_heads),
        out_specs=pl.BlockSpec((seq_len, HEAD_DIM), lambda b, h: (row_block0 + b, h)),
        out_shape=jax.ShapeDtypeStruct((qk.shape[1], n_q_heads * HEAD_DIM), BF16),
        scratch_shapes=[pltpu.VMEM((n_sub, tq, span), F32), pltpu.VMEM((n_sub, tq, span), BF16)],
        compiler_params=_params("parallel", "parallel"),
        name=name,
        **_into(out, in_specs, args),
    )(*args)


def _band_bias(tq, chunks, mult_of_offset):
    rows = np.arange(tq)[:, None]
    cols = np.concatenate([rel + np.arange(size) for rel, size in chunks])[None, :]
    mult = mult_of_offset(rows - cols)
    return jnp.asarray(np.where(mult > 0, np.log2(np.maximum(mult, 1).astype(np.float64)), NEG_INF), F32)


def _mult_a(d):
    return (np.abs(d) <= A_HALF_WINDOW).astype(np.int64)


def _mult_c(d):
    mult = np.zeros_like(d)
    for window, dil in C_BRANCHES:
        reach = (window // (2 * dil)) * dil
        mult = mult + ((np.abs(d) <= reach) & (d % dil == 0))
    return mult


def _diff_attn_kernel(q_ref, k_ref, v_ref, lq1_ref, lk1_ref, lq2_ref, lk2_ref, g_ref,
                      *maybe_out_alias_then_out_and_scratch, seq_len, tk, strip, lambda_init):
    o_ref, s_scr, p_scr, acc_scr, m_scr, l_scr, a_scr, knorm_scr = maybe_out_alias_then_out_and_scratch[-8:]
    tq = q_ref.shape[1]
    n_lane_blocks = tk // LANE
    n_chunks = seq_len // tk
    rows_per_pass = min(tq, strip)
    m_scr[...] = jnp.full(m_scr.shape, NEG_INF, F32)
    l_scr[...] = jnp.zeros(l_scr.shape, F32)
    acc_scr[...] = jnp.zeros(acc_scr.shape, F32)

    @pl.when(pl.program_id(2) == 0)
    def _():
        for c in range(2):
            def chunk_max(t, best):
                kk = k_ref[c, pl.ds(pl.multiple_of(t * tk, tk), tk), :].astype(F32)
                return jnp.maximum(best, jnp.max(jnp.sum(kk * kk, axis=-1, keepdims=True), axis=0, keepdims=True))
            best = lax.fori_loop(0, n_chunks, chunk_max, jnp.zeros((1, 1), F32))
            knorm_scr[c] = jnp.broadcast_to(best, knorm_scr.shape[1:])

    streams = [(c, slice(r0, r0 + rows_per_pass)) for c in range(2) for r0 in range(0, tq, rows_per_pass)]

    def scores(c, rows, k_start):
        s_scr[c, rows, :] = lax.dot_general(q_ref[c, rows, :], k_ref[c, pl.ds(k_start, tk), :], _NT,
                                            preferred_element_type=F32)

    def row_max(c, rows):
        mx = s_scr[c, rows, 0:LANE]
        for cb in range(1, n_lane_blocks):
            mx = jnp.maximum(mx, s_scr[c, rows, cb * LANE:(cb + 1) * LANE])
        m_old = m_scr[c, rows, :]
        m_new = jnp.maximum(m_old, jnp.max(mx, axis=-1, keepdims=True))
        a_scr[c, rows, :] = jnp.exp2(m_old - m_new)
        m_scr[c, rows, :] = m_new

    def probs_and_values(c, rows, k_start, rescale):
        m = m_scr[c, rows, :]
        lsum = l_scr[c, rows, :]
        if rescale:
            a = a_scr[c, rows, :]
            lsum = a * lsum
        for cb in range(n_lane_blocks):
            cols = slice(cb * LANE, (cb + 1) * LANE)
            p = jnp.exp2(s_scr[c, rows, cols] - m)
            lsum = lsum + p
            p_scr[c, rows, cols] = p.astype(BF16)
        l_scr[c, rows, :] = lsum
        pv = jnp.dot(p_scr[c, rows, :], v_ref[0, pl.ds(k_start, tk), :], preferred_element_type=F32)
        for e in range(v_ref.shape[2] // LANE):
            cols = slice(e * LANE, (e + 1) * LANE)
            old = acc_scr[c, rows, cols]
            acc_scr[c, rows, cols] = (a * old if rescale else old) + pv[:, cols]

    def chunk(kj, online):
        k_start = pl.multiple_of(kj * tk, tk)
        for c, rows in streams:
            scores(c, rows, k_start)
        if online:
            for c, rows in streams:
                row_max(c, rows)
        for c, rows in streams:
            probs_and_values(c, rows, k_start, rescale=online)

    def rest(online):
        def body(kj, carry):
            chunk(kj, online)
            return carry
        lax.fori_loop(1, n_chunks, body, 0)

    chunk(0, online=True)
    if n_chunks > 1:
        slack = jnp.zeros((1, 1), F32)
        for c in range(2):
            q = q_ref[c].astype(F32)
            bound = jnp.sqrt(jnp.sum(q * q, axis=-1, keepdims=True) * knorm_scr[c, 0:1, 0:1])
            slack = jnp.maximum(slack, jnp.max(bound - m_scr[c, :, 0:1], axis=0, keepdims=True))
        fixed_max_is_safe = slack[0, 0] <= MAX_FIXED_MAX_SLACK
        pl.when(fixed_max_is_safe)(lambda: rest(online=False))
        pl.when(jnp.logical_not(fixed_max_is_safe))(lambda: rest(online=True))

    lam = (jnp.exp(jnp.sum(lq1_ref[...] * lk1_ref[...], keepdims=True))
           - jnp.exp(jnp.sum(lq2_ref[...] * lk2_ref[...], keepdims=True)) + lambda_init)
    o1 = acc_scr[0] / jnp.sum(l_scr[0], axis=-1, keepdims=True)
    o2 = acc_scr[1] / jnp.sum(l_scr[1], axis=-1, keepdims=True)
    o = o1 - lam * o2
    o = o * lax.rsqrt(jnp.mean(o * o, axis=-1, keepdims=True) + LN_EPS)
    o_ref[...] = (o * g_ref[...] * (1.0 - lambda_init)).astype(BF16)


def _diff_attn(qk, v, lq1, lk1, lq2, lk2, subln_g, lambda_init, out, *, n_heads, seq_len, n_seq,
               row_block0, tq=1024, strip=512):
    tq = _tile(seq_len, tuple(t for t in (1024, 512, 256, 128) if t <= tq))
    tk = _tile(seq_len, (2048, 1024, 512, 256, 128) if seq_len * 1024 <= VMEM_LIMIT_BYTES // 8
               else (1024, 512, 256, 128))
    q_blocks = seq_len // tq
    vd = 2 * HEAD_DIM
    vec = lambda width: pl.BlockSpec((1, width), lambda b, h, i: (0, 0))
    q_row_block = lambda b, i: (row_block0 + b) * q_blocks + i
    in_specs = [pl.BlockSpec((2, tq, HEAD_DIM), lambda b, h, i: (h, q_row_block(b, i), 0)),
                pl.BlockSpec((2, seq_len, HEAD_DIM), lambda b, h, i: (n_heads + h, row_block0 + b, 0),
                             pipeline_mode=pl.Buffered(1)),
                pl.BlockSpec((1, seq_len, vd), lambda b, h, i: (h, row_block0 + b, 0),
                             pipeline_mode=pl.Buffered(1)),
                vec(HEAD_DIM), vec(HEAD_DIM), vec(HEAD_DIM), vec(HEAD_DIM), vec(vd)]
    args = [qk, qk, v, lq1.reshape(1, -1), lk1.reshape(1, -1), lq2.reshape(1, -1), lk2.reshape(1, -1),
            subln_g.reshape(1, -1)]
    return pl.pallas_call(
        functools.partial(_diff_attn_kernel, seq_len=seq_len, tk=tk, strip=strip, lambda_init=lambda_init),
        grid=(n_seq, n_heads, q_blocks),
        out_specs=pl.BlockSpec((tq, vd), lambda b, h, i: (q_row_block(b, i), h)),
        out_shape=jax.ShapeDtypeStruct((qk.shape[1], n_heads * vd), BF16),
        scratch_shapes=[pltpu.VMEM((2, tq, tk), F32), pltpu.VMEM((2, tq, tk), BF16),
                        pltpu.VMEM((2, tq, vd), F32), pltpu.VMEM((2, tq, LANE), F32),
                        pltpu.VMEM((2, tq, LANE), F32), pltpu.VMEM((2, tq, LANE), F32),
                        pltpu.VMEM((2, 8, LANE), F32)],
        compiler_params=_params("parallel", "parallel", "arbitrary"),
        name="diff_attn",
        **_into(out, in_specs, args),
    )(*args)


def _rope_tables(max_len, q_scale):
    half = ROT_DIM // 2
    inv_freq = ROPE_THETA ** (-(jnp.arange(0, ROT_DIM, 2, dtype=F32) / ROT_DIM))
    lane = jnp.arange(HEAD_DIM)
    inv_freq_of_lane = jnp.where(lane < ROT_DIM, inv_freq[lane % half], 0.0)
    ang = jnp.arange(max_len, dtype=F32)[:, None] * inv_freq_of_lane[None, :]
    cos, sin = jnp.cos(ang), jnp.sin(ang)
    s1 = jnp.where(lane < half, -sin, 0.0)
    s2 = jnp.where((lane >= half) & (lane < ROT_DIM), sin, 0.0)
    tab = jnp.stack([cos, s1, s2])
    return jnp.stack([tab * q_scale, tab])


def _seq_groups(seq_lens_and_counts):
    groups, row = [], 0
    for seq_len, n_seq in seq_lens_and_counts:
        assert row % seq_len == 0
        groups.append((seq_len, n_seq, row // seq_len))
        row += seq_len * n_seq
    return groups


def kernel(x_prompt, x_sample, a_w_qkv, a_w_o, a_sink, b_w_qkv, b_w_o, b_lambda_q1, b_lambda_k1,
           b_lambda_q2, b_lambda_k2, b_subln_g, c_w_qkv, c_w_o, ln1_g, ln1_b, ln2_g, ln2_b,
           w_gate, w_up, w_down):
    d = x_prompt.shape[-1]
    depth = ln1_g.shape[0]
    n_heads = d // HEAD_DIM
    n_kv_a = n_heads // A_GROUP
    b_heads = n_heads // 2
    alpha = (2.0 * depth) ** 0.25
    q_scale = HEAD_DIM ** -0.5 * LOG2_E

    groups = _seq_groups([(x_prompt.shape[1], x_prompt.shape[0]), (x_sample.shape[1], x_sample.shape[0])])
    x = jnp.concatenate([x_prompt.reshape(-1, d), x_sample.reshape(-1, d)], axis=0)
    xb = x.astype(BF16)
    tab = _rope_tables(max(seq_len for seq_len, _, _ in groups), q_scale)

    reach_c = max((w // (2 * dil)) * dil for w, dil in C_BRANCHES)
    a_w_qkv, a_w_o, b_w_qkv, b_w_o, c_w_qkv, c_w_o, w_gate, w_up, w_down = (
        w.astype(BF16) for w in (a_w_qkv, a_w_o, b_w_qkv, b_w_o, c_w_qkv, c_w_o, w_gate, w_up, w_down))

    for i in range(depth):
        kind, j = i % N_MIXERS, i // N_MIXERS
        attn = None
        if kind == 0:
            nq = n_heads * HEAD_DIM
            nk = n_kv_a * HEAD_DIM
            qk = _proj_heads(xb, a_w_qkv, j, 0, nq + nk, tab, n_q_cols=nq, groups=groups)
            v = _proj_heads(xb, a_w_qkv, j, nq + nk, nk)
            for sl, ns, rb in groups:
                attn = _window_attn(qk, v, a_sink[j], _mult_a, A_HALF_WINDOW, attn, n_q_heads=n_heads,
                                    group=A_GROUP, tq=256, n_sub=4, seq_len=sl, n_seq=ns, row_block0=rb,
                                    name="window_attn_sink")
            w_o = a_w_o
        elif kind == 1:
            lambda_init = 0.8 - 0.6 * math.exp(-0.3 * i)
            qk = _proj_heads(xb, b_w_qkv, j, 0, 2 * d, tab, n_q_cols=d, groups=groups)
            v = _proj_heads(xb, b_w_qkv, j, 2 * d, d, chunk=2 * HEAD_DIM)
            for sl, ns, rb in groups:
                attn = _diff_attn(qk, v, b_lambda_q1[j], b_lambda_k1[j], b_lambda_q2[j], b_lambda_k2[j],
                                  b_subln_g[j], lambda_init, attn, n_heads=b_heads, seq_len=sl, n_seq=ns,
                                  row_block0=rb)
            w_o = b_w_o
        else:
            qk = _proj_heads(xb, c_w_qkv, j, 0, 2 * d, tab, n_q_cols=d, groups=groups)
            v = _proj_heads(xb, c_w_qkv, j, 2 * d, d)
            for sl, ns, rb in groups:
                attn = _window_attn(qk, v, None, _mult_c, reach_c, attn, n_q_heads=n_heads, group=1,
                                    tq=256, n_sub=4, seq_len=sl, n_seq=ns, row_block0=rb,
                                    name="window_attn_dilated")
            w_o = c_w_o
        h = _matmul(attn, w_o, j)
        x, xb = _deepnorm_ln(x, h, ln1_g[i], ln1_b[i], alpha)

        hidden = _gateup(xb, w_gate, w_up, i)
        h = _matmul(hidden, w_down, i)
        if i + 1 < depth:
            x, xb = _deepnorm_ln(x, h, ln2_g[i], ln2_b[i], alpha)

    n_prompt = x_prompt.shape[0] * x_prompt.shape[1]
    y_prompt = _deepnorm_ln(x, h, ln2_g[-1], ln2_b[-1], alpha, rows=(0, n_prompt))
    y_sample = _deepnorm_ln(x, h, ln2_g[-1], ln2_b[-1], alpha, rows=(n_prompt, x.shape[0] - n_prompt))
    return y_prompt.reshape(x_prompt.shape), y_sample.reshape(x_sample.shape)
```

```python
import functools
import math

import numpy as np
import jax
import jax.numpy as jnp
from jax import lax
from jax.experimental import pallas as pl
from jax.experimental.pallas import tpu as pltpu

F32 = jnp.float32
BF16 = jnp.bfloat16

LANE = 128
HEAD_DIM = 128
LOG2_E = math.log2(math.e)
ROT_DIM = HEAD_DIM // 4
ROPE_THETA = 500000.0
LN_EPS = 1e-5
NEG_INF = -1e30
MAX_FIXED_MAX_SLACK = 80.0
N_MIXERS = 3
A_GROUP = 4
A_HALF_WINDOW = 128
C_BRANCHES = ((128, 1), (512, 4), (2048, 16))
VMEM_LIMIT_BYTES = 56 * 1024 * 1024

_NT = (((1,), (1,)), ((), ()))


def _params(*semantics):
    return pltpu.CompilerParams(dimension_semantics=semantics, vmem_limit_bytes=VMEM_LIMIT_BYTES)


def _tile(n, prefs):
    for t in prefs:
        if n % t == 0:
            return t
    raise ValueError(f"no tile in {prefs} divides {n}")


def _proj_rope_kernel(x_ref, w_ref, t_ref, o_ref):
    acc = jnp.dot(x_ref[...], w_ref[...], preferred_element_type=F32)
    cos, s1, s2 = t_ref[0, 0], t_ref[0, 1], t_ref[0, 2]
    half = ROT_DIM // 2
    for c in range(o_ref.shape[0]):
        blk = acc[:, c * HEAD_DIM:(c + 1) * HEAD_DIM]
        blk = (blk * cos + pltpu.roll(blk, HEAD_DIM - half, 1) * s1 + pltpu.roll(blk, half, 1) * s2)
        o_ref[c] = blk.astype(BF16)


def _proj_plain_kernel(x_ref, w_ref, o_ref):
    acc = jnp.dot(x_ref[...], w_ref[...], preferred_element_type=F32)
    chunk = o_ref.shape[2]
    for c in range(o_ref.shape[0]):
        o_ref[c] = acc[:, c * chunk:(c + 1) * chunk].astype(BF16)


def _proj_heads(xb, w, layer, col0, n, tab=None, n_q_cols=0, groups=(), chunk=HEAD_DIM):
    m, k = xb.shape
    tm = _tile(math.gcd(m, *(sl for sl, _, _ in groups)), (1024, 512, 256))
    tn = _tile(math.gcd(n, n_q_cols, col0), (1024, 512, 256, 128))
    grid = (m // tm, n // tn)
    col_tile0 = col0 // tn
    in_specs = [pl.BlockSpec((tm, k), lambda i, j: (i, 0)),
                pl.BlockSpec((None, k, tn), lambda i, j: (layer, 0, col_tile0 + j))]
    args = [xb, w]
    if tab is not None:
        assert chunk == HEAD_DIM
        n_q_tiles = n_q_cols // tn

        def position_tile(i):
            pos, tile0 = i, 0
            for seq_len, n_seq, _ in groups:
                pos = jnp.where(i >= tile0, (i - tile0) % (seq_len // tm), pos)
                tile0 += seq_len * n_seq // tm
            return pos

        in_specs.append(pl.BlockSpec((1, 3, tm, HEAD_DIM),
                                     lambda i, j: (jnp.where(j < n_q_tiles, 0, 1), 0, position_tile(i), 0)))
        args.append(tab)
        body = _proj_rope_kernel
    else:
        body = _proj_plain_kernel
    return pl.pallas_call(
        body,
        grid=grid,
        in_specs=in_specs,
        out_specs=pl.BlockSpec((tn // chunk, tm, chunk), lambda i, j: (j, i, 0)),
        out_shape=jax.ShapeDtypeStruct((n // chunk, m, chunk), BF16),
        compiler_params=_params("parallel", "parallel"),
        name="proj_heads_rope" if tab is not None else "proj_heads",
    )(*args)


def _mm_kernel(a_ref, w_ref, o_ref):
    o_ref[...] = jnp.dot(a_ref[...], w_ref[...], preferred_element_type=F32).astype(o_ref.dtype)


def _matmul(a, w, layer, out_dtype=F32):
    m, k = a.shape
    n = w.shape[2]
    if 8 * 1024 * k > VMEM_LIMIT_BYTES * 3 // 4:
        tm = _tile(m, (512, 256))
        tn = _tile(n, (1024, 512, 256, 128))
        return pl.pallas_call(
            _mm_kernel,
            grid=(n // tn, m // tm),
            in_specs=[pl.BlockSpec((tm, k), lambda j, i: (i, 0)),
                      pl.BlockSpec((None, k, tn), lambda j, i: (layer, 0, j), pipeline_mode=pl.Buffered(1))],
            out_specs=pl.BlockSpec((tm, tn), lambda j, i: (i, j)),
            out_shape=jax.ShapeDtypeStruct((m, n), out_dtype),
            compiler_params=_params("parallel", "parallel"),
            name="matmul_long_k",
        )(a, w)
    tm = _tile(m, (1024, 512, 256))
    tn = _tile(n, (1024, 512, 256, 128))
    return pl.pallas_call(
        _mm_kernel,
        grid=(m // tm, n // tn),
        in_specs=[pl.BlockSpec((tm, k), lambda i, j: (i, 0)),
                  pl.BlockSpec((None, k, tn), lambda i, j: (layer, 0, j))],
        out_specs=pl.BlockSpec((tm, tn), lambda i, j: (i, j)),
        out_shape=jax.ShapeDtypeStruct((m, n), out_dtype),
        compiler_params=_params("parallel", "parallel"),
        name="matmul",
    )(a, w)


def _gateup_kernel(x_ref, wg_ref, wu_ref, o_ref):
    x = x_ref[...]
    g = jnp.dot(x, wg_ref[...], preferred_element_type=F32)
    u = jnp.dot(x, wu_ref[...], preferred_element_type=F32)
    o_ref[...] = (g * jax.nn.sigmoid(g) * u).astype(BF16)


def _gateup(xb, wg, wu, layer):
    m, k = xb.shape
    f = wg.shape[2]
    tn = _tile(f, (512, 256, 128))
    tm = _tile(m, (2048, 1024, 512, 256) if tn <= 256 else (1024, 512, 256))
    return pl.pallas_call(
        _gateup_kernel,
        grid=(m // tm, f // tn),
        in_specs=[pl.BlockSpec((tm, k), lambda i, j: (i, 0)),
                  pl.BlockSpec((None, k, tn), lambda i, j: (layer, 0, j)),
                  pl.BlockSpec((None, k, tn), lambda i, j: (layer, 0, j))],
        out_specs=pl.BlockSpec((tm, tn), lambda i, j: (i, j)),
        out_shape=jax.ShapeDtypeStruct((m, f), BF16),
        compiler_params=_params("parallel", "parallel"),
        name="ffn_gate_up",
    )(xb, wg, wu)


def _ln_kernel(x_ref, h_ref, g_ref, b_ref, xo_ref, *maybe_xb_ref, alpha):
    y = alpha * x_ref[...] + h_ref[...]
    mu = jnp.mean(y, axis=-1, keepdims=True)
    yc = y - mu
    var = jnp.mean(yc * yc, axis=-1, keepdims=True)
    out = yc * lax.rsqrt(var + LN_EPS) * g_ref[...] + b_ref[...]
    xo_ref[...] = out
    for xb_ref in maybe_xb_ref:
        xb_ref[...] = out.astype(BF16)


def _deepnorm_ln(x, h, g, b, alpha, rows=None):
    m, d = x.shape
    row0, n_rows = rows or (0, m)
    tm = _tile(math.gcd(row0, n_rows), (256, 128))
    tile0 = row0 // tm
    row_in = pl.BlockSpec((tm, d), lambda i: (tile0 + i, 0))
    row_out = pl.BlockSpec((tm, d), lambda i: (i, 0))
    vec = pl.BlockSpec((1, d), lambda i: (0, 0))
    out_dtypes = (F32,) if rows else (F32, BF16)
    outs = pl.pallas_call(
        functools.partial(_ln_kernel, alpha=alpha),
        grid=(n_rows // tm,),
        in_specs=[row_in, row_in, vec, vec],
        out_specs=[row_out] * len(out_dtypes),
        out_shape=[jax.ShapeDtypeStruct((n_rows, d), dt) for dt in out_dtypes],
        compiler_params=_params("parallel"),
        name="deepnorm_ln",
    )(x, h, g.reshape(1, d), b.reshape(1, d))
    return outs[0] if rows else outs


def _into(out, in_specs, args):
    if out is None:
        return dict(in_specs=in_specs)
    args.append(out)
    return dict(in_specs=in_specs + [pl.BlockSpec(memory_space=pl.ANY)],
                input_output_aliases={len(args) - 1: 0})


def _window_attn_kernel(sink_ref, q_ref, k_ref, v_ref, bias_ref, *maybe_out_alias_then_out_and_scratch,
                        seq_len, tq, chunks, has_sink):
    o_ref, s_scr, p_scr = maybe_out_alias_then_out_and_scratch[-3:]
    head = pl.program_id(1)
    n_sub = s_scr.shape[0]
    n_lane_blocks = s_scr.shape[2] // LANE

    def scores(sub, q_start):
        q = q_ref[0, pl.ds(q_start, tq), :]
        starts = []
        col = 0
        for rel, size in chunks:
            start = q_start + rel
            inside = jnp.logical_and(start >= 0, start + size <= seq_len)
            cstart = pl.multiple_of(jnp.clip(start, 0, seq_len - size), HEAD_DIM)
            starts.append(cstart)
            kj = k_ref[0, pl.ds(cstart, size), :]
            sj = lax.dot_general(q, kj, _NT, preferred_element_type=F32)
            pen = jnp.where(inside, 0.0, NEG_INF).astype(F32)
            s_scr[sub, :, col:col + size] = sj + bias_ref[:, col:col + size] + pen
            col += size
        return starts

    def softmax(sub):
        m = s_scr[sub, :, 0:LANE]
        for cb in range(1, n_lane_blocks):
            m = jnp.maximum(m, s_scr[sub, :, cb * LANE:(cb + 1) * LANE])
        m = jnp.broadcast_to(jnp.max(m, axis=-1, keepdims=True), (tq, LANE))
        if has_sink:
            sk = sink_ref[head] * LOG2_E
            m = jnp.maximum(m, sk)
        lsum = jnp.zeros((tq, LANE), F32)
        for cb in range(n_lane_blocks):
            cols = slice(cb * LANE, (cb + 1) * LANE)
            p = jnp.exp2(s_scr[sub, :, cols] - m)
            lsum = lsum + p
            p_scr[sub, :, cols] = p.astype(BF16)
        denom = jnp.sum(lsum, axis=-1, keepdims=True)
        if has_sink:
            denom = denom + jnp.exp2(sk - m[:, 0:1])
        return denom

    def weighted_values(sub, q_start, starts, denom):
        acc = jnp.zeros((tq, HEAD_DIM), F32)
        col = 0
        for (rel, size), cstart in zip(chunks, starts):
            vj = v_ref[0, pl.ds(cstart, size), :]
            acc = acc + jnp.dot(p_scr[sub, :, col:col + size], vj, preferred_element_type=F32)
            col += size
        o_ref[pl.ds(q_start, tq), :] = (acc / denom).astype(BF16)

    def body(qi, carry):
        q_starts = [pl.multiple_of((qi * n_sub + sub) * tq, tq) for sub in range(n_sub)]
        starts = [scores(sub, q_starts[sub]) for sub in range(n_sub)]
        denoms = [softmax(sub) for sub in range(n_sub)]
        for sub in range(n_sub):
            weighted_values(sub, q_starts[sub], starts[sub], denoms[sub])
        return carry

    lax.fori_loop(0, seq_len // (n_sub * tq), body, 0)


def _window_attn(qk, v, sink, mult_of_offset, reach, out, *, n_q_heads, group, tq, n_sub, seq_len, n_seq,
                 row_block0, name):
    has_sink = sink is not None
    if not has_sink:
        sink = jnp.zeros((n_q_heads,), F32)
    piece = min(reach, tq)
    assert reach % piece == 0 and tq % piece == 0 and seq_len % (n_sub * tq) == 0
    side = reach // piece
    chunks = (tuple((-reach + i * piece, piece) for i in range(side)) + ((0, tq),)
              + tuple((tq + i * piece, piece) for i in range(side)))
    bias = _band_bias(tq, chunks, mult_of_offset)
    span = bias.shape[1]
    seq_spec = lambda head_of: pl.BlockSpec((1, seq_len, HEAD_DIM),
                                            lambda b, h: (head_of(h), row_block0 + b, 0))
    in_specs = [pl.BlockSpec(memory_space=pltpu.SMEM),
                seq_spec(lambda h: h),
                seq_spec(lambda h: n_q_heads + h // group),
                seq_spec(lambda h: h // group),
                pl.BlockSpec((tq, span), lambda b, h: (0, 0), pipeline_mode=pl.Buffered(1))]
    args = [sink, qk, qk, v, bias]
    return pl.pallas_call(
        functools.partial(_window_attn_kernel, seq_len=seq_len, tq=tq, chunks=chunks, has_sink=has_sink),
        grid=(n_seq, n_q_heads),
        out_specs=pl.BlockSpec((seq_len, HEAD_DIM), lambda b, h: (row_block0 + b, h)),
        out_shape=jax.ShapeDtypeStruct((qk.shape[1], n_q_heads * HEAD_DIM), BF16),
        scratch_shapes=[pltpu.VMEM((n_sub, tq, span), F32), pltpu.VMEM((n_sub, tq, span), BF16)],
        compiler_params=_params("parallel", "parallel"),
        name=name,
        **_into(out, in_specs, args),
    )(*args)


def _band_bias(tq, chunks, mult_of_offset):
    rows = np.arange(tq)[:, None]
    cols = np.concatenate([rel + np.arange(size) for rel, size in chunks])[None, :]
    mult = mult_of_offset(rows - cols)
    return jnp.asarray(np.where(mult > 0, np.log2(np.maximum(mult, 1).astype(np.float64)), NEG_INF), F32)


def _mult_a(d):
    return (np.abs(d) <= A_HALF_WINDOW).astype(np.int64)


def _mult_c(d):
    mult = np.zeros_like(d)
    for window, dil in C_BRANCHES:
        reach = (window // (2 * dil)) * dil
        mult = mult + ((np.abs(d) <= reach) & (d % dil == 0))
    return mult


def _diff_attn_kernel(q_ref, k_ref, v_ref, lq1_ref, lk1_ref, lq2_ref, lk2_ref, g_ref,
                      *maybe_out_alias_then_out_and_scratch, seq_len, tk, lambda_init):
    o_ref, s_scr, p_scr, acc_scr, m_scr, l_scr, a_scr, knorm_scr = maybe_out_alias_then_out_and_scratch[-8:]
    tq = q_ref.shape[1]
    n_lane_blocks = tk // LANE
    n_chunks = seq_len // tk
    m_scr[...] = jnp.full(m_scr.shape, NEG_INF, F32)
    l_scr[...] = jnp.zeros(l_scr.shape, F32)
    acc_scr[...] = jnp.zeros(acc_scr.shape, F32)

    @pl.when(pl.program_id(2) == 0)
    def _():
        for c in range(2):
            def chunk_max(t, best):
                kk = k_ref[c, pl.ds(pl.multiple_of(t * tk, tk), tk), :].astype(F32)
                return jnp.maximum(best, jnp.max(jnp.sum(kk * kk, axis=-1, keepdims=True), axis=0, keepdims=True))
            best = lax.fori_loop(0, n_chunks, chunk_max, jnp.zeros((1, 1), F32))
            knorm_scr[c] = jnp.broadcast_to(best, knorm_scr.shape[1:])

    streams = [(c, slice(0, tq)) for c in range(2)]

    def scores(c, rows, k_start):
        s_scr[c, rows, :] = lax.dot_general(q_ref[c, rows, :], k_ref[c, pl.ds(k_start, tk), :], _NT,
                                            preferred_element_type=F32)

    def row_max(c, rows):
        mx = s_scr[c, rows, 0:LANE]
        for cb in range(1, n_lane_blocks):
            mx = jnp.maximum(mx, s_scr[c, rows, cb * LANE:(cb + 1) * LANE])
        m_old = m_scr[c, rows, :]
        m_new = jnp.maximum(m_old, jnp.max(mx, axis=-1, keepdims=True))
        a_scr[c, rows, :] = jnp.exp2(m_old - m_new)
        m_scr[c, rows, :] = m_new

    def probs_and_values(c, rows, k_start, rescale):
        m = m_scr[c, rows, :]
        lsum = l_scr[c, rows, :]
        if rescale:
            a = a_scr[c, rows, :]
            lsum = a * lsum
        for cb in range(n_lane_blocks):
            cols = slice(cb * LANE, (cb + 1) * LANE)
            p = jnp.exp2(s_scr[c, rows, cols] - m)
            lsum = lsum + p
            p_scr[c, rows, cols] = p.astype(BF16)
        l_scr[c, rows, :] = lsum
        pv = jnp.dot(p_scr[c, rows, :], v_ref[0, pl.ds(k_start, tk), :], preferred_element_type=F32)
        for e in range(v_ref.shape[2] // LANE):
            cols = slice(e * LANE, (e + 1) * LANE)
            old = acc_scr[c, rows, cols]
            acc_scr[c, rows, cols] = (a * old if rescale else old) + pv[:, cols]

    def chunk(kj, online):
        k_start = pl.multiple_of(kj * tk, tk)
        for c, rows in streams:
            scores(c, rows, k_start)
        if online:
            for c, rows in streams:
                row_max(c, rows)
        for c, rows in streams:
            probs_and_values(c, rows, k_start, rescale=online)

    def rest(online):
        def body(kj, carry):
            chunk(kj, online)
            return carry
        lax.fori_loop(1, n_chunks, body, 0)

    chunk(0, online=True)
    if n_chunks > 1:
        slack = jnp.zeros((1, 1), F32)
        for c in range(2):
            q = q_ref[c].astype(F32)
            bound = jnp.sqrt(jnp.sum(q * q, axis=-1, keepdims=True) * knorm_scr[c, 0:1, 0:1])
            slack = jnp.maximum(slack, jnp.max(bound - m_scr[c, :, 0:1], axis=0, keepdims=True))
        fixed_max_is_safe = slack[0, 0] <= MAX_FIXED_MAX_SLACK
        pl.when(fixed_max_is_safe)(lambda: rest(online=False))
        pl.when(jnp.logical_not(fixed_max_is_safe))(lambda: rest(online=True))

    lam = (jnp.exp(jnp.sum(lq1_ref[...] * lk1_ref[...], keepdims=True))
           - jnp.exp(jnp.sum(lq2_ref[...] * lk2_ref[...], keepdims=True)) + lambda_init)
    o1 = acc_scr[0] / jnp.sum(l_scr[0], axis=-1, keepdims=True)
    o2 = acc_scr[1] / jnp.sum(l_scr[1], axis=-1, keepdims=True)
    o = o1 - lam * o2
    o = o * lax.rsqrt(jnp.mean(o * o, axis=-1, keepdims=True) + LN_EPS)
    o_ref[...] = (o * g_ref[...] * (1.0 - lambda_init)).astype(BF16)


def _diff_attn(qk, v, lq1, lk1, lq2, lk2, subln_g, lambda_init, out, *, n_heads, seq_len, n_seq,
               row_block0):
    tq = _tile(seq_len, (1024, 512, 256, 128))
    tk = _tile(seq_len, (2048, 1024, 512, 256, 128) if seq_len * 1024 <= VMEM_LIMIT_BYTES // 8
               else (1024, 512, 256, 128))
    q_blocks = seq_len // tq
    vd = 2 * HEAD_DIM
    vec = lambda width: pl.BlockSpec((1, width), lambda b, h, i: (0, 0))
    q_row_block = lambda b, i: (row_block0 + b) * q_blocks + i
    in_specs = [pl.BlockSpec((2, tq, HEAD_DIM), lambda b, h, i: (h, q_row_block(b, i), 0)),
                pl.BlockSpec((2, seq_len, HEAD_DIM), lambda b, h, i: (n_heads + h, row_block0 + b, 0),
                             pipeline_mode=pl.Buffered(1)),
                pl.BlockSpec((1, seq_len, vd), lambda b, h, i: (h, row_block0 + b, 0),
                             pipeline_mode=pl.Buffered(1)),
                vec(HEAD_DIM), vec(HEAD_DIM), vec(HEAD_DIM), vec(HEAD_DIM), vec(vd)]
    args = [qk, qk, v, lq1.reshape(1, -1), lk1.reshape(1, -1), lq2.reshape(1, -1), lk2.reshape(1, -1),
            subln_g.reshape(1, -1)]
    return pl.pallas_call(
        functools.partial(_diff_attn_kernel, seq_len=seq_len, tk=tk, lambda_init=lambda_init),
        grid=(n_seq, n_heads, q_blocks),
        out_specs=pl.BlockSpec((tq, vd), lambda b, h, i: (q_row_block(b, i), h)),
        out_shape=jax.ShapeDtypeStruct((qk.shape[1], n_heads * vd), BF16),
        scratch_shapes=[pltpu.VMEM((2, tq, tk), F32), pltpu.VMEM((2, tq, tk), BF16),
                        pltpu.VMEM((2, tq, vd), F32), pltpu.VMEM((2, tq, LANE), F32),
                        pltpu.VMEM((2, tq, LANE), F32), pltpu.VMEM((2, tq, LANE), F32),
                        pltpu.VMEM((2, 8, LANE), F32)],
        compiler_params=_params("parallel", "parallel", "arbitrary"),
        name="diff_attn",
        **_into(out, in_specs, args),
    )(*args)


def _rope_tables(max_len, q_scale):
    half = ROT_DIM // 2
    inv_freq = ROPE_THETA ** (-(jnp.arange(0, ROT_DIM, 2, dtype=F32) / ROT_DIM))
    lane = jnp.arange(HEAD_DIM)
    inv_freq_of_lane = jnp.where(lane < ROT_DIM, inv_freq[lane % half], 0.0)
    ang = jnp.arange(max_len, dtype=F32)[:, None] * inv_freq_of_lane[None, :]
    cos, sin = jnp.cos(ang), jnp.sin(ang)
    s1 = jnp.where(lane < half, -sin, 0.0)
    s2 = jnp.where((lane >= half) & (lane < ROT_DIM), sin, 0.0)
    tab = jnp.stack([cos, s1, s2])
    return jnp.stack([tab * q_scale, tab])


def _seq_groups(seq_lens_and_counts):
    groups, row = [], 0
    for seq_len, n_seq in seq_lens_and_counts:
        assert row % seq_len == 0
        groups.append((seq_len, n_seq, row // seq_len))
        row += seq_len * n_seq
    return groups


def kernel(x_prompt, x_sample, a_w_qkv, a_w_o, a_sink, b_w_qkv, b_w_o, b_lambda_q1, b_lambda_k1,
           b_lambda_q2, b_lambda_k2, b_subln_g, c_w_qkv, c_w_o, ln1_g, ln1_b, ln2_g, ln2_b,
           w_gate, w_up, w_down):
    d = x_prompt.shape[-1]
    depth = ln1_g.shape[0]
    n_heads = d // HEAD_DIM
    n_kv_a = n_heads // A_GROUP
    b_heads = n_heads // 2
    alpha = (2.0 * depth) ** 0.25
    q_scale = HEAD_DIM ** -0.5 * LOG2_E

    groups = _seq_groups([(x_prompt.shape[1], x_prompt.shape[0]), (x_sample.shape[1], x_sample.shape[0])])
    x = jnp.concatenate([x_prompt.reshape(-1, d), x_sample.reshape(-1, d)], axis=0)
    xb = x.astype(BF16)
    tab = _rope_tables(max(seq_len for seq_len, _, _ in groups), q_scale)

    reach_c = max((w // (2 * dil)) * dil for w, dil in C_BRANCHES)
    a_w_qkv, a_w_o, b_w_qkv, b_w_o, c_w_qkv, c_w_o, w_gate, w_up, w_down = (
        w.astype(BF16) for w in (a_w_qkv, a_w_o, b_w_qkv, b_w_o, c_w_qkv, c_w_o, w_gate, w_up, w_down))

    for i in range(depth):
        kind, j = i % N_MIXERS, i // N_MIXERS
        attn = None
        if kind == 0:
            nq = n_heads * HEAD_DIM
            nk = n_kv_a * HEAD_DIM
            qk = _proj_heads(xb, a_w_qkv, j, 0, nq + nk, tab, n_q_cols=nq, groups=groups)
            v = _proj_heads(xb, a_w_qkv, j, nq + nk, nk)
            for sl, ns, rb in groups:
                attn = _window_attn(qk, v, a_sink[j], _mult_a, A_HALF_WINDOW, attn, n_q_heads=n_heads,
                                    group=A_GROUP, tq=256, n_sub=4, seq_len=sl, n_seq=ns, row_block0=rb,
                                    name="window_attn_sink")
            w_o = a_w_o
        elif kind == 1:
            lambda_init = 0.8 - 0.6 * math.exp(-0.3 * i)
            qk = _proj_heads(xb, b_w_qkv, j, 0, 2 * d, tab, n_q_cols=d, groups=groups)
            v = _proj_heads(xb, b_w_qkv, j, 2 * d, d, chunk=2 * HEAD_DIM)
            for sl, ns, rb in groups:
                attn = _diff_attn(qk, v, b_lambda_q1[j], b_lambda_k1[j], b_lambda_q2[j], b_lambda_k2[j],
                                  b_subln_g[j], lambda_init, attn, n_heads=b_heads, seq_len=sl, n_seq=ns,
                                  row_block0=rb)
            w_o = b_w_o
        else:
            qk = _proj_heads(xb, c_w_qkv, j, 0, 2 * d, tab, n_q_cols=d, groups=groups)
            v = _proj_heads(xb, c_w_qkv, j, 2 * d, d)
            for sl, ns, rb in groups:
                attn = _window_attn(qk, v, None, _mult_c, reach_c, attn, n_q_heads=n_heads, group=1,
                                    tq=256, n_sub=4, seq_len=sl, n_seq=ns, row_block0=rb,
                                    name="window_attn_dilated")
            w_o = c_w_o
        h = _matmul(attn, w_o, j)
        x, xb = _deepnorm_ln(x, h, ln1_g[i], ln1_b[i], alpha)

        hidden = _gateup(xb, w_gate, w_up, i)
        h = _matmul(hidden, w_down, i)
        if i + 1 < depth:
            x, xb = _deepnorm_ln(x, h, ln2_g[i], ln2_b[i], alpha)

    n_prompt = x_prompt.shape[0] * x_prompt.shape[1]
    y_prompt = _deepnorm_ln(x, h, ln2_g[-1], ln2_b[-1], alpha, rows=(0, n_prompt))
    y_sample = _deepnorm_ln(x, h, ln2_g[-1], ln2_b[-1], alpha, rows=(n_prompt, x.shape[0] - n_prompt))
    return y_prompt.reshape(x_prompt.shape), y_sample.reshape(x_sample.shape)
```
